```python
import jax, jax.numpy as jnp
from jax import lax
import numpy as np

D_MODEL = 2048
BATCH = 1
SEQ = 16384
DEPTH = 2
DEC_BATCH = 2
DEC_SEQ = 16384
PAST_LEN = 128

GRID_W = 64
N_MIXERS = 2
N_A_LAYERS = (DEPTH + 1) // 2
N_B_LAYERS = DEPTH // 2
NA_HEADS = 16
NA_HEAD_DIM = D_MODEL // NA_HEADS
NA_KR_MAX = 8
NA_KC = 16
MLA_HEADS = 16
MLA_Q_RANK = 384
MLA_KV_RANK = 512
MLA_NOPE = 128
MLA_ROPE = 64
MLA_V = 128
ROPE_THETA = 10000.0
D_FF = -(-8 * D_MODEL // (3 * 256)) * 256
Q_BLOCK = 128
EPS = 1e-6

kernel_name = "hybrid_natten_mla_encoder"


def rmsnorm(x, g):
    xf = x.astype(jnp.float32)
    y = xf * lax.rsqrt(jnp.mean(xf * xf, axis=-1, keepdims=True) + EPS)
    return (y * g.astype(jnp.float32)).astype(x.dtype)


def neighbourhood_attention(x, w_qkv, b_qkv, rpb, w_o):
    B, T, D = x.shape
    rows = T // GRID_W
    kr = min(NA_KR_MAX, rows)
    qkv = (x @ w_qkv + b_qkv).reshape(B, rows, GRID_W, 3, NA_HEADS, NA_HEAD_DIM)
    q = qkv[:, :, :, 0] * (NA_HEAD_DIM ** -0.5)
    k = qkv[:, :, :, 1]
    v = qkv[:, :, :, 2]
    cols = jnp.arange(GRID_W)
    c_start = jnp.clip(cols - NA_KC // 2, 0, GRID_W - NA_KC)
    c_idx = c_start[:, None] + jnp.arange(NA_KC)[None, :]
    c_off = c_idx - cols[:, None] + (NA_KC - 1)
    q_rows = jnp.moveaxis(q, 1, 0)

    def row_block(args):
        r, q_r = args
        r_start = jnp.clip(r - kr // 2, 0, rows - kr)
        k_r = lax.dynamic_slice_in_dim(k, r_start, kr, axis=1)
        v_r = lax.dynamic_slice_in_dim(v, r_start, kr, axis=1)
        k_g = k_r[:, :, c_idx]
        v_g = v_r[:, :, c_idx]
        r_off = r_start + jnp.arange(kr) - r + (NA_KR_MAX - 1)
        bias = rpb[:, r_off[None, :, None], c_off[:, None, :]]
        s = jnp.einsum('bqhd,brqkhd->bhqrk', q_r, k_g).astype(jnp.float32) + bias.astype(jnp.float32)
        p = jax.nn.softmax(s.reshape(B, NA_HEADS, GRID_W, kr * NA_KC), axis=-1)
        p = p.reshape(s.shape).astype(v.dtype)
        return jnp.einsum('bhqrk,brqkhd->bqhd', p, v_g)

    o = lax.map(row_block, (jnp.arange(rows), q_rows))
    o = jnp.moveaxis(o, 0, 1).reshape(B, T, D)
    return o @ w_o


def rope_tables(T):
    inv_freq = ROPE_THETA ** (-jnp.arange(0, MLA_ROPE, 2, dtype=jnp.float32) / MLA_ROPE)
    ang = jnp.arange(T, dtype=jnp.float32)[:, None] * inv_freq[None, :]
    return jnp.cos(ang), jnp.sin(ang)


def apply_rope(x, cos, sin):
    half = x.shape[-1] // 2
    x1, x2 = x[..., :half], x[..., half:]
    cos = cos.astype(x.dtype)
    sin = sin.astype(x.dtype)
    return jnp.concatenate([x1 * cos - x2 * sin, x1 * sin + x2 * cos], axis=-1)


def latent_attention(x, w_dq, g_q, w_uq, w_dkv, g_kv, w_kr, w_ukv, w_o):
    B, T, D = x.shape
    H = MLA_HEADS
    c_q = rmsnorm(x @ w_dq, g_q)
    q = (c_q @ w_uq).reshape(B, T, H, MLA_NOPE + MLA_ROPE)
    q_nope, q_rope = q[..., :MLA_NOPE], q[..., MLA_NOPE:]
    c_kv = rmsnorm(x @ w_dkv, g_kv)
    kv = (c_kv @ w_ukv).reshape(B, T, H, MLA_NOPE + MLA_V)
    k_nope, v = kv[..., :MLA_NOPE], kv[..., MLA_NOPE:]
    k_rope = x @ w_kr
    cos, sin = rope_tables(T)
    q_rope = apply_rope(q_rope, cos[None, :, None], sin[None, :, None])
    k_rope = apply_rope(k_rope, cos[None], sin[None])
    scale = (MLA_NOPE + MLA_ROPE) ** -0.5
    nb = T // Q_BLOCK
    qn = jnp.moveaxis((q_nope * scale).reshape(B, nb, Q_BLOCK, H, MLA_NOPE), 1, 0)
    qr = jnp.moveaxis((q_rope * scale).reshape(B, nb, Q_BLOCK, H, MLA_ROPE), 1, 0)

    def q_block(args):
        qn_b, qr_b = args
        s = (jnp.einsum('bqhd,bkhd->bhqk', qn_b, k_nope)
             + jnp.einsum('bqhr,bkr->bhqk', qr_b, k_rope))
        p = jax.nn.softmax(s.astype(jnp.float32), axis=-1).astype(v.dtype)
        return jnp.einsum('bhqk,bkhd->bqhd', p, v)

    o = lax.map(q_block, (qn, qr))
    o = jnp.moveaxis(o, 0, 1).reshape(B, T, H * MLA_V)
    return o @ w_o


def swiglu(x, w_gate, w_up, w_down):
    return (jax.nn.silu(x @ w_gate) * (x @ w_up)) @ w_down


def trunk(x, g_mix, g_ffn, g_final, na_w_qkv, na_b_qkv, na_rpb, na_w_o,
          mla_w_dq, mla_g_q, mla_w_uq, mla_w_dkv, mla_g_kv, mla_w_kr, mla_w_ukv, mla_w_o,
          ffn_w_gate, ffn_w_up, ffn_w_down):
    for i in range(DEPTH):
        h = rmsnorm(x, g_mix[i])
        j = i // N_MIXERS
        if i % N_MIXERS == 0:
            h = neighbourhood_attention(h, na_w_qkv[j], na_b_qkv[j], na_rpb[j], na_w_o[j])
        else:
            h = latent_attention(h, mla_w_dq[j], mla_g_q[j], mla_w_uq[j], mla_w_dkv[j],
                                 mla_g_kv[j], mla_w_kr[j], mla_w_ukv[j], mla_w_o[j])
        x = x + h
        x = x + swiglu(rmsnorm(x, g_ffn[i]), ffn_w_gate[i], ffn_w_up[i], ffn_w_down[i])
    return rmsnorm(x, g_final)


def setup_inputs(seed: int = 0) -> dict:
    key = jax.random.key(seed)
    ks = jax.random.split(key, 24)
    f32 = jnp.float32

    def nrm(k, shape, fan_in):
        return jax.random.normal(k, shape, f32) * (fan_in ** -0.5)

    def gain(k, shape):
        return 1.0 + 0.05 * jax.random.normal(k, shape, f32)

    D, H = D_MODEL, MLA_HEADS
    return {
        "x_prompt": jax.random.normal(ks[0], (BATCH, SEQ, D), f32),
        "x_sample": jax.random.normal(ks[1], (DEC_BATCH, DEC_SEQ, D), f32),
        "g_mix": gain(ks[2], (DEPTH, D)),
        "g_ffn": gain(ks[3], (DEPTH, D)),
        "g_final": gain(ks[4], (D,)),
        "na_w_qkv": nrm(ks[5], (N_A_LAYERS, D, 3 * D), D),
        "na_b_qkv": 0.02 * jax.random.normal(ks[6], (N_A_LAYERS, 3 * D), f32),
        "na_rpb": 0.1 * jax.random.normal(ks[7], (N_A_LAYERS, NA_HEADS, 2 * NA_KR_MAX - 1, 2 * NA_KC - 1), f32),
        "na_w_o": nrm(ks[8], (N_A_LAYERS, D, D), D),
        "mla_w_dq": nrm(ks[9], (N_B_LAYERS, D, MLA_Q_RANK), D),
        "mla_g_q": gain(ks[10], (N_B_LAYERS, MLA_Q_RANK)),
        "mla_w_uq": nrm(ks[11], (N_B_LAYERS, MLA_Q_RANK, H * (MLA_NOPE + MLA_ROPE)), MLA_Q_RANK),
        "mla_w_dkv": nrm(ks[12], (N_B_LAYERS, D, MLA_KV_RANK), D),
        "mla_g_kv": gain(ks[13], (N_B_LAYERS, MLA_KV_RANK)),
        "mla_w_kr": nrm(ks[14], (N_B_LAYERS, D, MLA_ROPE), D),
        "mla_w_ukv": nrm(ks[15], (N_B_LAYERS, MLA_KV_RANK, H * (MLA_NOPE + MLA_V)), MLA_KV_RANK),
        "mla_w_o": nrm(ks[16], (N_B_LAYERS, H * MLA_V, D), H * MLA_V),
        "ffn_w_gate": nrm(ks[17], (DEPTH, D, D_FF), D),
        "ffn_w_up": nrm(ks[18], (DEPTH, D, D_FF), D),
        "ffn_w_down": nrm(ks[19], (DEPTH, D_FF, D), D_FF),
    }


def reference(x_prompt, x_sample, g_mix, g_ffn, g_final, na_w_qkv, na_b_qkv, na_rpb, na_w_o,
              mla_w_dq, mla_g_q, mla_w_uq, mla_w_dkv, mla_g_kv, mla_w_kr, mla_w_ukv, mla_w_o,
              ffn_w_gate, ffn_w_up, ffn_w_down):
    y_prompt = trunk(x_prompt, g_mix, g_ffn, g_final, na_w_qkv, na_b_qkv, na_rpb, na_w_o,
                     mla_w_dq, mla_g_q, mla_w_uq, mla_w_dkv, mla_g_kv, mla_w_kr, mla_w_ukv, mla_w_o,
                     ffn_w_gate, ffn_w_up, ffn_w_down)
    y_sample = trunk(x_sample, g_mix, g_ffn, g_final, na_w_qkv, na_b_qkv, na_rpb, na_w_o,
                     mla_w_dq, mla_g_q, mla_w_uq, mla_w_dkv, mla_g_kv, mla_w_kr, mla_w_ukv, mla_w_o,
                     ffn_w_gate, ffn_w_up, ffn_w_down)
    return (y_prompt, y_sample)
```

```python
import functools

import jax
import jax.numpy as jnp
from jax import lax
from jax.experimental import pallas as pl
from jax.experimental.pallas import tpu as pltpu

GRID_W = 64
NA_HEADS = 16
NA_KR = 8
NA_KC = 16
MLA_HEADS = 16
MLA_NOPE = 128
MLA_ROPE = 64
MLA_V = 128
ROPE_THETA = 10000.0
EPS = 1e-6

LANES = 128
BF16_SUBLANES = 16
VMEM_LIMIT_BYTES = 56 * 1024 * 1024

NA_QROWS = 8
NA_KROWS = 16
NA_KBLK_ROWS = 4
MASK_VALUE = -1e30

MLA_QK = 256
MLA_VT_ROWS = MLA_V + BF16_SUBLANES

F32 = jnp.float32
BF16 = jnp.bfloat16


def _params(*sem):
    return pltpu.CompilerParams(dimension_semantics=sem, vmem_limit_bytes=VMEM_LIMIT_BYTES)


def _rmsnorm_f32(x, g):
    return x * lax.rsqrt(jnp.mean(x * x, axis=-1, keepdims=True) + EPS) * g


def _dot(a, b):
    return jnp.dot(a, b, preferred_element_type=F32)


def _dot_nt(a, b):
    return lax.dot_general(a, b, (((1,), (1,)), ((), ())), preferred_element_type=F32)


def _qkv_kernel(x_ref, g_ref, w_ref, b_ref, cs_ref, o_ref, h_ref):
    @pl.when(pl.program_id(1) == 0)
    def _():
        h_ref[...] = _rmsnorm_f32(x_ref[...], g_ref[...]).astype(BF16)

    acc = (_dot(h_ref[...], w_ref[...]) + b_ref[...]) * cs_ref[...]
    for hh in range(o_ref.shape[0]):
        o_ref[hh] = acc[:, hh * LANES:(hh + 1) * LANES].astype(BF16)


def qkv_proj(x, g, w, b, cs, *, tm, tn):
    n, d = x.shape
    m = w.shape[1]
    return pl.pallas_call(
        _qkv_kernel,
        grid=(n // tm, m // tn),
        in_specs=[
            pl.BlockSpec((tm, d), lambda i, j: (i, 0)),
            pl.BlockSpec((1, d), lambda i, j: (0, 0)),
            pl.BlockSpec((d, tn), lambda i, j: (0, j)),
            pl.BlockSpec((1, tn), lambda i, j: (0, j)),
            pl.BlockSpec((1, tn), lambda i, j: (0, j)),
        ],
        out_specs=pl.BlockSpec((tn // LANES, tm, LANES), lambda i, j: (j, i, 0)),
        out_shape=jax.ShapeDtypeStruct((m // LANES, n, LANES), BF16),
        scratch_shapes=[pltpu.VMEM((tm, d), BF16)],
        compiler_params=_params("parallel", "arbitrary"),
        name="qkv_proj",
    )(x, g, w, b, cs)


def _na_kernel(q_ref, k0, k1, k2, k3, v0, v1, v2, v3, bias_ref, o_ref):
    q = q_ref[0]
    k = jnp.concatenate([k0[0], k1[0], k2[0], k3[0]], axis=0)
    v = jnp.concatenate([v0[0], v1[0], v2[0], v3[0]], axis=0)
    s = _dot_nt(q, k) + bias_ref[0, 0]
    m = jnp.max(s, axis=-1, keepdims=True)
    p = jnp.exp(s - m)
    l = jnp.sum(p, axis=-1, keepdims=True)
    o = _dot(p.astype(BF16), v)
    o_ref[0] = (o / l).astype(BF16)


def na_bias_table(rpb, rows):
    heads = rpb.shape[0]
    nb = rows // NA_QROWS
    cols = jnp.arange(GRID_W)
    c_start = jnp.clip(cols - NA_KC // 2, 0, GRID_W - NA_KC)
    cvalid = (cols[None, :] >= c_start[:, None]) & (cols[None, :] < c_start[:, None] + NA_KC)
    coff = jnp.clip(cols[None, :] - cols[:, None] + (NA_KC - 1), 0, 2 * NA_KC - 2)
    a = rpb[:, :, coff]
    tables = []
    for b in (0, 1, nb - 1):
        r = b * NA_QROWS + jnp.arange(NA_QROWS)
        r_start = jnp.clip(r - NA_KR // 2, 0, rows - NA_KR)
        kb = min(max(b * NA_QROWS - NA_KR // 2, 0), rows - NA_KROWS)
        krow = kb + jnp.arange(NA_KROWS)
        rvalid = (krow[None, :] >= r_start[:, None]) & (krow[None, :] < r_start[:, None] + NA_KR)
        roff = jnp.clip(krow[None, :] - r[:, None] + (NA_KR - 1), 0, 2 * NA_KR - 2)
        t = a[:, roff]
        valid = rvalid[:, :, None, None] & cvalid[None, None]
        t = jnp.where(valid[None], t, MASK_VALUE)
        t = t.transpose(0, 1, 3, 2, 4).reshape(heads, NA_QROWS * GRID_W, NA_KROWS * GRID_W)
        tables.append(t)
    return jnp.stack(tables).astype(F32)


def na_attention(qkv_hm, bias, *, n_seq, seq_len):
    heads = NA_HEADS
    rows = seq_len // GRID_W
    assert rows % NA_QROWS == 0 and rows >= 3 * NA_QROWS
    nb = rows // NA_QROWS
    tq = NA_QROWS * GRID_W
    tkb = NA_KBLK_ROWS * GRID_W
    n_kblk = NA_KROWS // NA_KBLK_ROWS
    kblk_per_seq = seq_len // tkb
    n = n_seq * seq_len

    def q_map(h, s, b):
        return (h, s * nb + b, 0)

    def kv_map(which, j):
        def f(h, s, b):
            base = jnp.clip(2 * b - 1, 0, kblk_per_seq - n_kblk)
            return (which * heads + h, s * kblk_per_seq + base + j, 0)
        return f

    def bias_map(h, s, b):
        cls = jnp.where(b == 0, 0, jnp.where(b == nb - 1, 2, 1))
        return (cls, h, 0, 0)

    in_specs = [pl.BlockSpec((1, tq, LANES), q_map)]
    in_specs += [pl.BlockSpec((1, tkb, LANES), kv_map(1, j)) for j in range(n_kblk)]
    in_specs += [pl.BlockSpec((1, tkb, LANES), kv_map(2, j)) for j in range(n_kblk)]
    in_specs += [pl.BlockSpec((1, 1, tq, NA_KROWS * GRID_W), bias_map)]
    return pl.pallas_call(
        _na_kernel,
        grid=(heads, n_seq, nb),
        in_specs=in_specs,
        out_specs=pl.BlockSpec((1, tq, LANES), q_map),
        out_shape=jax.ShapeDtypeStruct((heads, n, LANES), BF16),
        compiler_params=_params("parallel", "parallel", "arbitrary"),
        name="na_attention",
    )(*([qkv_hm] * (1 + 2 * n_kblk)), bias)


def _oproj_kernel(o_ref, w_ref, x_ref, y_ref):
    o = jnp.concatenate([o_ref[h] for h in range(o_ref.shape[0])], axis=-1)
    y_ref[...] = x_ref[...] + _dot(o, w_ref[...])


def out_proj(o_hm, w, x, *, tm):
    heads, n, dh = o_hm.shape
    d = w.shape[1]
    return pl.pallas_call(
        _oproj_kernel,
        grid=(n // tm,),
        in_specs=[
            pl.BlockSpec((heads, tm, dh), lambda i: (0, i, 0)),
            pl.BlockSpec((heads * dh, d), lambda i: (0, 0)),
            pl.BlockSpec((tm, d), lambda i: (i, 0)),
        ],
        out_specs=pl.BlockSpec((tm, d), lambda i: (i, 0)),
        out_shape=jax.ShapeDtypeStruct((n, d), F32),
        compiler_params=_params("parallel"),
        name="out_proj",
    )(o_hm, w, x)


def _ffn_up_kernel(x_ref, g_ref, wg_ref, wu_ref, a_ref, h_ref):
    @pl.when(pl.program_id(1) == 0)
    def _():
        h_ref[...] = _rmsnorm_f32(x_ref[...], g_ref[...]).astype(BF16)

    h = h_ref[...]
    gate = _dot(h, wg_ref[...])
    up = _dot(h, wu_ref[...])
    a_ref[...] = (gate * jax.nn.sigmoid(gate) * up).astype(BF16)


def ffn_up(x, g, wg, wu, *, tm, tn):
    n, d = x.shape
    f = wg.shape[1]
    return pl.pallas_call(
        _ffn_up_kernel,
        grid=(n // tm, f // tn),
        in_specs=[
            pl.BlockSpec((tm, d), lambda i, j: (i, 0)),
            pl.BlockSpec((1, d), lambda i, j: (0, 0)),
            pl.BlockSpec((d, tn), lambda i, j: (0, j)),
            pl.BlockSpec((d, tn), lambda i, j: (0, j)),
        ],
        out_specs=pl.BlockSpec((tm, tn), lambda i, j: (i, j)),
        out_shape=jax.ShapeDtypeStruct((n, f), BF16),
        scratch_shapes=[pltpu.VMEM((tm, d), BF16)],
        compiler_params=_params("parallel", "arbitrary"),
        name="ffn_up",
    )(x, g, wg, wu)


def _ffn_down_kernel(a_ref, w_ref, x_ref, y_ref):
    y_ref[...] = x_ref[...] + _dot(a_ref[...], w_ref[...])


def ffn_down(a, w, x, *, tm, tn):
    n, f = a.shape
    d = w.shape[1]
    return pl.pallas_call(
        _ffn_down_kernel,
        grid=(d // tn, n // tm),
        in_specs=[
            pl.BlockSpec((tm, f), lambda j, i: (i, 0)),
            pl.BlockSpec((f, tn), lambda j, i: (0, j)),
            pl.BlockSpec((tm, tn), lambda j, i: (i, j)),
        ],
        out_specs=pl.BlockSpec((tm, tn), lambda j, i: (i, j)),
        out_shape=jax.ShapeDtypeStruct((n, d), F32),
        compiler_params=_params("parallel", "parallel"),
        name="ffn_down",
    )(a, w, x)


def _rope_pair(g128, c2):
    prod = g128 * c2
    both = prod + pltpu.roll(prod, LANES // 2, axis=1)
    lane = lax.broadcasted_iota(jnp.int32, both.shape, 1)
    return jnp.where(lane < MLA_ROPE, both, 0.0)


def _mla_down_kernel(x_ref, g_ref, w_ref, gq_ref, gkv_ref, c2_ref, cq_ref, ckv_ref, kr_ref, *, q_rank, kv_rank):
    h = _rmsnorm_f32(x_ref[...], g_ref[...]).astype(BF16)
    c = _dot(h, w_ref[...])
    cq_ref[...] = _rmsnorm_f32(c[:, :q_rank], gq_ref[...]).astype(BF16)
    ckv_ref[...] = _rmsnorm_f32(c[:, q_rank:q_rank + kv_rank], gkv_ref[...]).astype(BF16)
    kr_ref[...] = _rope_pair(c[:, q_rank + kv_rank:], c2_ref[...]).astype(BF16)


def mla_down(x, g, w, gq, gkv, c2, *, tm, seq_len):
    n, d = x.shape
    q_rank, kv_rank = gq.shape[1], gkv.shape[1]
    width = w.shape[1]
    assert width == q_rank + kv_rank + LANES
    pos_blocks = seq_len // tm
    return pl.pallas_call(
        functools.partial(_mla_down_kernel, q_rank=q_rank, kv_rank=kv_rank),
        grid=(n // tm,),
        in_specs=[
            pl.BlockSpec((tm, d), lambda i: (i, 0)),
            pl.BlockSpec((1, d), lambda i: (0, 0)),
            pl.BlockSpec((d, width), lambda i: (0, 0)),
            pl.BlockSpec((1, q_rank), lambda i: (0, 0)),
            pl.BlockSpec((1, kv_rank), lambda i: (0, 0)),
            pl.BlockSpec((tm, LANES), lambda i: (i % pos_blocks, 0)),
        ],
        out_specs=[
            pl.BlockSpec((tm, q_rank), lambda i: (i, 0)),
            pl.BlockSpec((tm, kv_rank), lambda i: (i, 0)),
            pl.BlockSpec((tm, LANES), lambda i: (i, 0)),
        ],
        out_shape=[
            jax.ShapeDtypeStruct((n, q_rank), BF16),
            jax.ShapeDtypeStruct((n, kv_rank), BF16),
            jax.ShapeDtypeStruct((n, LANES), BF16),
        ],
        compiler_params=_params("parallel"),
        name="mla_down",
    )(x, g, w, gq, gkv, c2)


def _mla_q_kernel(cq_ref, w_ref, c2_ref, q_ref, *, scale):
    acc = _dot(cq_ref[...], w_ref[...])
    c2 = c2_ref[...]
    for hh in range(q_ref.shape[0]):
        base = hh * MLA_QK
        q_ref[hh, :, :MLA_NOPE] = (acc[:, base:base + MLA_NOPE] * scale).astype(BF16)
        rope = _rope_pair(acc[:, base + MLA_NOPE:base + MLA_QK], c2)
        q_ref[hh, :, MLA_NOPE:] = (rope * scale).astype(BF16)


def mla_q_proj(cq, w, c2, *, tm, heads_per_step, seq_len, scale):
    n, q_rank = cq.shape
    heads = w.shape[1] // MLA_QK
    tn = heads_per_step * MLA_QK
    pos_blocks = seq_len // tm
    return pl.pallas_call(
        functools.partial(_mla_q_kernel, scale=scale),
        grid=(n // tm, heads // heads_per_step),
        in_specs=[
            pl.BlockSpec((tm, q_rank), lambda i, j: (i, 0)),
            pl.BlockSpec((q_rank, tn), lambda i, j: (0, j)),
            pl.BlockSpec((tm, LANES), lambda i, j: (i % pos_blocks, 0)),
        ],
        out_specs=pl.BlockSpec((heads_per_step, tm, MLA_QK), lambda i, j: (j, i, 0)),
        out_shape=jax.ShapeDtypeStruct((heads, n, MLA_QK), BF16),
        compiler_params=_params("parallel", "parallel"),
        name="mla_q_proj",
    )(cq, w, c2)


def _mla_kv_kernel(ckv_ref, kr_ref, wk_ref, wvt_ref, k_ref, vt_ref):
    ckv = ckv_ref[...]
    kn = _dot(ckv, wk_ref[...])
    vt = _dot_nt(wvt_ref[...], ckv)
    kr = kr_ref[...]
    tm = ckv.shape[0]
    row = lax.broadcasted_iota(jnp.int32, (BF16_SUBLANES, tm), 0)
    ones_rows = jnp.where(row == 0, 1.0, 0.0).astype(BF16)
    for hh in range(k_ref.shape[0]):
        k_ref[hh, :, :MLA_NOPE] = kn[:, hh * MLA_NOPE:(hh + 1) * MLA_NOPE].astype(BF16)
        k_ref[hh, :, MLA_NOPE:] = kr
        vt_ref[hh, :MLA_V, :] = vt[hh * MLA_V:(hh + 1) * MLA_V, :].astype(BF16)
        vt_ref[hh, MLA_V:, :] = ones_rows


def mla_kv_proj(ckv, kr, wk, wvt, *, tm, heads_per_step):
    n, kv_rank = ckv.shape
    heads = wk.shape[1] // MLA_NOPE
    hps = heads_per_step
    return pl.pallas_call(
        _mla_kv_kernel,
        grid=(n // tm, heads // hps),
        in_specs=[
            pl.BlockSpec((tm, kv_rank), lambda i, j: (i, 0)),
            pl.BlockSpec((tm, LANES), lambda i, j: (i, 0)),
            pl.BlockSpec((kv_rank, hps * MLA_NOPE), lambda i, j: (0, j)),
            pl.BlockSpec((hps * MLA_V, kv_rank), lambda i, j: (j, 0)),
        ],
        out_specs=[
            pl.BlockSpec((hps, tm, MLA_QK), lambda i, j: (j, i, 0)),
            pl.BlockSpec((hps, MLA_VT_ROWS, tm), lambda i, j: (j, 0, i)),
        ],
        out_shape=[
            jax.ShapeDtypeStruct((heads, n, MLA_QK), BF16),
            jax.ShapeDtypeStruct((heads, MLA_VT_ROWS, n), BF16),
        ],
        compiler_params=_params("parallel", "parallel"),
        name="mla_kv_proj",
    )(ckv, kr, wk, wvt)


def _mla_attn_kernel(q_ref, k_ref, vt_ref, o_ref, m_ref, acc_ref, *, tk):
    n_kt = k_ref.shape[1] // tk
    q = q_ref[0]
    m_ref[...] = jnp.full(m_ref.shape, MASK_VALUE, F32)
    acc_ref[...] = jnp.zeros(acc_ref.shape, F32)

    def body(kt, carry):
        start = pl.multiple_of(kt * tk, tk)
        k = k_ref[0, pl.ds(start, tk), :]
        vt = vt_ref[0, :, pl.ds(start, tk)]
        s = _dot_nt(k, q)
        m_prev = m_ref[...]
        m_new = jnp.maximum(m_prev, jnp.max(s, axis=0, keepdims=True))
        p = jnp.exp(s - m_new).astype(BF16)
        alpha = jnp.exp(m_prev - m_new)
        acc_ref[...] = acc_ref[...] * alpha + _dot(vt, p)
        m_ref[...] = m_new
        return carry

    lax.fori_loop(0, n_kt, body, 0)
    acc = acc_ref[...]
    o_t = acc[:MLA_V, :] / acc[MLA_V:MLA_V + 1, :]
    o_ref[0] = o_t.T.astype(BF16)


def mla_attention(q_hm, k_hm, vt_hm, *, n_seq, seq_len, tq, tk):
    heads, n, _ = q_hm.shape
    nq = seq_len // tq
    return pl.pallas_call(
        functools.partial(_mla_attn_kernel, tk=tk),
        grid=(n_seq, heads, nq),
        in_specs=[
            pl.BlockSpec((1, tq, MLA_QK), lambda s, h, i: (h, s * nq + i, 0)),
            pl.BlockSpec((1, seq_len, MLA_QK), lambda s, h, i: (h, s, 0)),
            pl.BlockSpec((1, MLA_VT_ROWS, seq_len), lambda s, h, i: (h, 0, s)),
        ],
        out_specs=pl.BlockSpec((1, tq, MLA_V), lambda s, h, i: (h, s * nq + i, 0)),
        out_shape=jax.ShapeDtypeStruct((heads, n, MLA_V), BF16),
        scratch_shapes=[pltpu.VMEM((1, tq), F32), pltpu.VMEM((MLA_VT_ROWS, tq), F32)],
        compiler_params=_params("parallel", "parallel", "arbitrary"),
        name="mla_attention",
    )(q_hm, k_hm, vt_hm)


def _norm_kernel(x_ref, g_ref, y_ref):
    y_ref[...] = _rmsnorm_f32(x_ref[...], g_ref[...])


def final_norm(x, g, *, tm):
    n, d = x.shape
    return pl.pallas_call(
        _norm_kernel,
        grid=(n // tm,),
        in_specs=[pl.BlockSpec((tm, d), lambda i: (i, 0)), pl.BlockSpec((1, d), lambda i: (0, 0))],
        out_specs=pl.BlockSpec((tm, d), lambda i: (i, 0)),
        out_shape=jax.ShapeDtypeStruct((n, d), F32),
        compiler_params=_params("parallel"),
        name="final_norm",
    )(x, g)


def rope_table(seq_len):
    inv_freq = ROPE_THETA ** (-jnp.arange(0, MLA_ROPE, 2, dtype=F32) / MLA_ROPE)
    ang = jnp.arange(seq_len, dtype=F32)[:, None] * inv_freq[None, :]
    cos, sin = jnp.cos(ang), jnp.sin(ang)
    return jnp.concatenate([cos, cos, -sin, sin], axis=-1)


def _swap_halves(w):
    half = w.shape[-1] // 2
    return jnp.concatenate([w[..., half:], w[..., :half]], axis=-1)


def trunk(x, n_seq, seq_len, p, tiles):
    n, d = x.shape
    rows = seq_len // GRID_W
    row2 = lambda v: v.reshape(1, -1)

    w_qkv = p["na_w_qkv"][0].astype(BF16)
    na_dh = d // NA_HEADS
    col_scale = jnp.concatenate([jnp.full((d,), na_dh ** -0.5, F32), jnp.ones((2 * d,), F32)])
    qkv = qkv_proj(x, row2(p["g_mix"][0]), w_qkv, row2(p["na_b_qkv"][0]), row2(col_scale),
                   tm=tiles["tm"], tn=tiles["qkv_tn"])
    bias = na_bias_table(p["na_rpb"][0], rows)
    o = na_attention(qkv, bias, n_seq=n_seq, seq_len=seq_len)
    x = out_proj(o, p["na_w_o"][0].astype(BF16), x, tm=tiles["tm"])
    x = _ffn(x, p, 0, tiles)

    w_kr = p["mla_w_kr"][0]
    w_down = jnp.concatenate([p["mla_w_dq"][0], p["mla_w_dkv"][0], w_kr, _swap_halves(w_kr)], axis=1).astype(BF16)
    c2 = rope_table(seq_len)
    cq, ckv, kr = mla_down(x, row2(p["g_mix"][1]), w_down, row2(p["mla_g_q"][0]), row2(p["mla_g_kv"][0]), c2,
                           tm=tiles["tm"], seq_len=seq_len)
    q_rank = cq.shape[1]
    w_uq = p["mla_w_uq"][0].reshape(q_rank, MLA_HEADS, MLA_NOPE + MLA_ROPE)
    w_uq = jnp.concatenate([w_uq, _swap_halves(w_uq[..., MLA_NOPE:])], axis=-1)
    w_uq = w_uq.reshape(q_rank, MLA_HEADS * MLA_QK).astype(BF16)
    q_hm = mla_q_proj(cq, w_uq, c2, tm=tiles["tm"], heads_per_step=tiles["q_hps"], seq_len=seq_len,
                      scale=(MLA_NOPE + MLA_ROPE) ** -0.5)
    kv_rank = ckv.shape[1]
    w_ukv = p["mla_w_ukv"][0].reshape(kv_rank, MLA_HEADS, MLA_NOPE + MLA_V)
    w_uk = w_ukv[..., :MLA_NOPE].reshape(kv_rank, MLA_HEADS * MLA_NOPE).astype(BF16)
    w_uvt = w_ukv[..., MLA_NOPE:].reshape(kv_rank, MLA_HEADS * MLA_V).T.astype(BF16)
    k_hm, vt_hm = mla_kv_proj(ckv, kr, w_uk, w_uvt, tm=tiles["tm"], heads_per_step=tiles["kv_hps"])
    o = mla_attention(q_hm, k_hm, vt_hm, n_seq=n_seq, seq_len=seq_len, tq=tiles["mla_tq"], tk=tiles["mla_tk"])
    x = out_proj(o, p["mla_w_o"][0].astype(BF16), x, tm=tiles["tm"])
    x = _ffn(x, p, 1, tiles)
    return final_norm(x, row2(p["g_final"]), tm=tiles["tm"])


def _ffn(x, p, layer, tiles):
    a = ffn_up(x, p["g_ffn"][layer].reshape(1, -1), p["ffn_w_gate"][layer].astype(BF16),
               p["ffn_w_up"][layer].astype(BF16), tm=tiles["ffn_tm"], tn=tiles["ffn_tn"])
    return ffn_down(a, p["ffn_w_down"][layer].astype(BF16), x, tm=tiles["tm"], tn=tiles["down_tn"])


TILES = dict(tm=512, qkv_tn=1024, ffn_tm=1024, ffn_tn=512, down_tn=1024, q_hps=4, kv_hps=8,
             mla_tq=1024, mla_tk=1024)


def kernel(x_prompt, x_sample, g_mix, g_ffn, g_final, na_w_qkv, na_b_qkv, na_rpb, na_w_o, mla_w_dq, mla_g_q,
           mla_w_uq, mla_w_dkv, mla_g_kv, mla_w_kr, mla_w_ukv, mla_w_o, ffn_w_gate, ffn_w_up, ffn_w_down):
    p = dict(g_mix=g_mix, g_ffn=g_ffn, g_final=g_final, na_w_qkv=na_w_qkv, na_b_qkv=na_b_qkv, na_rpb=na_rpb,
             na_w_o=na_w_o, mla_w_dq=mla_w_dq, mla_g_q=mla_g_q, mla_w_uq=mla_w_uq, mla_w_dkv=mla_w_dkv,
             mla_g_kv=mla_g_kv, mla_w_kr=mla_w_kr, mla_w_ukv=mla_w_ukv, mla_w_o=mla_w_o,
             ffn_w_gate=ffn_w_gate, ffn_w_up=ffn_w_up, ffn_w_down=ffn_w_down)
    bp, seq_len, d = x_prompt.shape
    bs = x_sample.shape[0]
    assert x_sample.shape[1:] == (seq_len, d)
    x = jnp.concatenate([x_prompt.reshape(bp * seq_len, d), x_sample.reshape(bs * seq_len, d)], axis=0)
    y = trunk(x, bp + bs, seq_len, p, TILES)
    y = y.reshape(bp + bs, seq_len, d)
    return (y[:bp], y[bp:])
```

```python
import functools

import jax
import jax.numpy as jnp
from jax import lax
from jax.experimental import pallas as pl
from jax.experimental.pallas import tpu as pltpu

GRID_W = 64
NA_HEADS = 16
NA_KR = 8
NA_KC = 16
MLA_HEADS = 16
MLA_NOPE = 128
MLA_ROPE = 64
MLA_V = 128
ROPE_THETA = 10000.0
EPS = 1e-6
LOG2_E = 1.4426950408889634

LANES = 128
BF16_SUBLANES = 16
VMEM_LIMIT_BYTES = 56 * 1024 * 1024

NA_QROWS = 8
NA_KROWS = 16
NA_KBLK_ROWS = 4
MASK_VALUE = -1e30

MLA_QK = 256
MLA_VT_ROWS = MLA_V + BF16_SUBLANES

F32 = jnp.float32
BF16 = jnp.bfloat16


def _params(*sem):
    return pltpu.CompilerParams(dimension_semantics=sem, vmem_limit_bytes=VMEM_LIMIT_BYTES)


def _rmsnorm_f32(x, g):
    return x * lax.rsqrt(jnp.mean(x * x, axis=-1, keepdims=True) + EPS) * g


def _dot(a, b):
    return jnp.dot(a, b, preferred_element_type=F32)


def _dot_nt(a, b):
    return lax.dot_general(a, b, (((1,), (1,)), ((), ())), preferred_element_type=F32)


def _qkv_kernel(x_ref, g_ref, w_ref, b_ref, cs_ref, o_ref, h_ref):
    @pl.when(pl.program_id(1) == 0)
    def _():
        h_ref[...] = _rmsnorm_f32(x_ref[...], g_ref[...]).astype(BF16)

    acc = (_dot(h_ref[...], w_ref[...]) + b_ref[...]) * cs_ref[...]
    for hh in range(o_ref.shape[0]):
        o_ref[hh] = acc[:, hh * LANES:(hh + 1) * LANES].astype(BF16)


def qkv_proj(x, g, w, b, cs, *, tm, tn):
    n, d = x.shape
    m = w.shape[1]
    return pl.pallas_call(
        _qkv_kernel,
        grid=(n // tm, m // tn),
        in_specs=[
            pl.BlockSpec((tm, d), lambda i, j: (i, 0)),
            pl.BlockSpec((1, d), lambda i, j: (0, 0)),
            pl.BlockSpec((d, tn), lambda i, j: (0, j)),
            pl.BlockSpec((1, tn), lambda i, j: (0, j)),
            pl.BlockSpec((1, tn), lambda i, j: (0, j)),
        ],
        out_specs=pl.BlockSpec((tn // LANES, tm, LANES), lambda i, j: (j, i, 0)),
        out_shape=jax.ShapeDtypeStruct((m // LANES, n, LANES), BF16),
        scratch_shapes=[pltpu.VMEM((tm, d), BF16)],
        compiler_params=_params("parallel", "arbitrary"),
        name="qkv_proj",
    )(x, g, w, b, cs)


def _na_kernel(q_ref, k0, k1, k2, k3, v0, v1, v2, v3, bias_ref, o_ref):
    k = jnp.concatenate([k0[0], k1[0], k2[0], k3[0]], axis=0)
    v = jnp.concatenate([v0[0], v1[0], v2[0], v3[0]], axis=0)
    half = q_ref.shape[1] // 2
    for c in range(2):
        rows = pl.ds(c * half, half)
        s = _dot_nt(q_ref[0, rows, :], k) + bias_ref[0, 0, rows, :]
        m = jnp.max(s, axis=-1, keepdims=True)
        p = jnp.exp(s - m)
        l = jnp.sum(p, axis=-1, keepdims=True)
        o = _dot(p.astype(BF16), v)
        o_ref[0, rows, :] = (o / l).astype(BF16)


def na_bias_table(rpb, rows):
    heads = rpb.shape[0]
    nb = rows // NA_QROWS
    cols = jnp.arange(GRID_W)
    c_start = jnp.clip(cols - NA_KC // 2, 0, GRID_W - NA_KC)
    cvalid = (cols[None, :] >= c_start[:, None]) & (cols[None, :] < c_start[:, None] + NA_KC)
    coff = jnp.clip(cols[None, :] - cols[:, None] + (NA_KC - 1), 0, 2 * NA_KC - 2)
    a = rpb[:, :, coff]
    tables = []
    for b in (0, 1, nb - 1):
        r = b * NA_QROWS + jnp.arange(NA_QROWS)
        r_start = jnp.clip(r - NA_KR // 2, 0, rows - NA_KR)
        kb = min(max(b * NA_QROWS - NA_KR // 2, 0), rows - NA_KROWS)
        krow = kb + jnp.arange(NA_KROWS)
        rvalid = (krow[None, :] >= r_start[:, None]) & (krow[None, :] < r_start[:, None] + NA_KR)
        roff = jnp.clip(krow[None, :] - r[:, None] + (NA_KR - 1), 0, 2 * NA_KR - 2)
        t = a[:, roff]
        valid = rvalid[:, :, None, None] & cvalid[None, None]
        t = jnp.where(valid[None], t, MASK_VALUE)
        t = t.transpose(0, 1, 3, 2, 4).reshape(heads, NA_QROWS * GRID_W, NA_KROWS * GRID_W)
        tables.append(t)
    return jnp.stack(tables).astype(F32)


def na_attention(qkv_hm, bias, *, n_seq, seq_len):
    heads = NA_HEADS
    rows = seq_len // GRID_W
    assert rows % NA_QROWS == 0 and rows >= 3 * NA_QROWS
    nb = rows // NA_QROWS
    tq = NA_QROWS * GRID_W
    tkb = NA_KBLK_ROWS * GRID_W
    n_kblk = NA_KROWS // NA_KBLK_ROWS
    kblk_per_seq = seq_len // tkb
    n = n_seq * seq_len

    def q_map(h, s, b):
        return (h, s * nb + b, 0)

    def kv_map(which, j):
        def f(h, s, b):
            base = jnp.clip(2 * b - 1, 0, kblk_per_seq - n_kblk)
            return (which * heads + h, s * kblk_per_seq + base + j, 0)
        return f

    def bias_map(h, s, b):
        cls = jnp.where(b == 0, 0, jnp.where(b == nb - 1, 2, 1))
        return (cls, h, 0, 0)

    in_specs = [pl.BlockSpec((1, tq, LANES), q_map)]
    in_specs += [pl.BlockSpec((1, tkb, LANES), kv_map(1, j)) for j in range(n_kblk)]
    in_specs += [pl.BlockSpec((1, tkb, LANES), kv_map(2, j)) for j in range(n_kblk)]
    in_specs += [pl.BlockSpec((1, 1, tq, NA_KROWS * GRID_W), bias_map)]
    return pl.pallas_call(
        _na_kernel,
        grid=(heads, n_seq, nb),
        in_specs=in_specs,
        out_specs=pl.BlockSpec((1, tq, LANES), q_map),
        out_shape=jax.ShapeDtypeStruct((heads, n, LANES), BF16),
        compiler_params=_params("parallel", "parallel", "arbitrary"),
        name="na_attention",
    )(*([qkv_hm] * (1 + 2 * n_kblk)), bias)


def _oproj_kernel(o_ref, w_ref, x_ref, y_ref):
    o = jnp.concatenate([o_ref[h] for h in range(o_ref.shape[0])], axis=-1)
    y_ref[...] = x_ref[...] + _dot(o, w_ref[...])


def out_proj(o_hm, w, x, *, tm):
    heads, n, dh = o_hm.shape
    d = w.shape[1]
    return pl.pallas_call(
        _oproj_kernel,
        grid=(n // tm,),
        in_specs=[
            pl.BlockSpec((heads, tm, dh), lambda i: (0, i, 0)),
            pl.BlockSpec((heads * dh, d), lambda i: (0, 0)),
            pl.BlockSpec((tm, d), lambda i: (i, 0)),
        ],
        out_specs=pl.BlockSpec((tm, d), lambda i: (i, 0)),
        out_shape=jax.ShapeDtypeStruct((n, d), F32),
        compiler_params=_params("parallel"),
        name="out_proj",
    )(o_hm, w, x)


def _ffn_up_kernel(x_ref, g_ref, wg_ref, wu_ref, a_ref, h_ref):
    @pl.when(pl.program_id(1) == 0)
    def _():
        h_ref[...] = _rmsnorm_f32(x_ref[...], g_ref[...]).astype(BF16)

    h = h_ref[...]
    gate = _dot(h, wg_ref[...])
    up = _dot(h, wu_ref[...])
    a_ref[...] = (gate * jax.nn.sigmoid(gate) * up).astype(BF16)


def ffn_up(x, g, wg, wu, *, tm, tn):
    n, d = x.shape
    f = wg.shape[1]
    return pl.pallas_call(
        _ffn_up_kernel,
        grid=(n // tm, f // tn),
        in_specs=[
            pl.BlockSpec((tm, d), lambda i, j: (i, 0)),
            pl.BlockSpec((1, d), lambda i, j: (0, 0)),
            pl.BlockSpec((d, tn), lambda i, j: (0, j)),
            pl.BlockSpec((d, tn), lambda i, j: (0, j)),
        ],
        out_specs=pl.BlockSpec((tm, tn), lambda i, j: (i, j)),
        out_shape=jax.ShapeDtypeStruct((n, f), BF16),
        scratch_shapes=[pltpu.VMEM((tm, d), BF16)],
        compiler_params=_params("parallel", "arbitrary"),
        name="ffn_up",
    )(x, g, wg, wu)


def _ffn_down_kernel(a_ref, w_ref, x_ref, y_ref):
    y_ref[...] = x_ref[...] + _dot(a_ref[...], w_ref[...])


def ffn_down(a, w, x, *, tm, tn):
    n, f = a.shape
    d = w.shape[1]
    return pl.pallas_call(
        _ffn_down_kernel,
        grid=(d // tn, n // tm),
        in_specs=[
            pl.BlockSpec((tm, f), lambda j, i: (i, 0)),
            pl.BlockSpec((f, tn), lambda j, i: (0, j)),
            pl.BlockSpec((tm, tn), lambda j, i: (i, j)),
        ],
        out_specs=pl.BlockSpec((tm, tn), lambda j, i: (i, j)),
        out_shape=jax.ShapeDtypeStruct((n, d), F32),
        compiler_params=_params("parallel", "parallel"),
        name="ffn_down",
    )(a, w, x)


def _rope_pair(g128, c2):
    prod = g128 * c2
    both = prod + pltpu.roll(prod, LANES // 2, axis=1)
    lane = lax.broadcasted_iota(jnp.int32, both.shape, 1)
    return jnp.where(lane < MLA_ROPE, both, 0.0)


def _mla_down_kernel(x_ref, g_ref, w_ref, gq_ref, gkv_ref, c2_ref, cq_ref, ckv_ref, kr_ref, *, q_rank, kv_rank):
    h = _rmsnorm_f32(x_ref[...], g_ref[...]).astype(BF16)
    c = _dot(h, w_ref[...])
    cq_ref[...] = _rmsnorm_f32(c[:, :q_rank], gq_ref[...]).astype(BF16)
    ckv_ref[...] = _rmsnorm_f32(c[:, q_rank:q_rank + kv_rank], gkv_ref[...]).astype(BF16)
    kr_ref[...] = _rope_pair(c[:, q_rank + kv_rank:], c2_ref[...]).astype(BF16)


def mla_down(x, g, w, gq, gkv, c2, *, tm, seq_len):
    n, d = x.shape
    q_rank, kv_rank = gq.shape[1], gkv.shape[1]
    width = w.shape[1]
    assert width == q_rank + kv_rank + LANES
    pos_blocks = seq_len // tm
    return pl.pallas_call(
        functools.partial(_mla_down_kernel, q_rank=q_rank, kv_rank=kv_rank),
        grid=(n // tm,),
        in_specs=[
            pl.BlockSpec((tm, d), lambda i: (i, 0)),
            pl.BlockSpec((1, d), lambda i: (0, 0)),
            pl.BlockSpec((d, width), lambda i: (0, 0)),
            pl.BlockSpec((1, q_rank), lambda i: (0, 0)),
            pl.BlockSpec((1, kv_rank), lambda i: (0, 0)),
            pl.BlockSpec((tm, LANES), lambda i: (i % pos_blocks, 0)),
        ],
        out_specs=[
            pl.BlockSpec((tm, q_rank), lambda i: (i, 0)),
            pl.BlockSpec((tm, kv_rank), lambda i: (i, 0)),
            pl.BlockSpec((tm, LANES), lambda i: (i, 0)),
        ],
        out_shape=[
            jax.ShapeDtypeStruct((n, q_rank), BF16),
            jax.ShapeDtypeStruct((n, kv_rank), BF16),
            jax.ShapeDtypeStruct((n, LANES), BF16),
        ],
        compiler_params=_params("parallel"),
        name="mla_down",
    )(x, g, w, gq, gkv, c2)


def _mla_q_kernel(cq_ref, w_ref, c2_ref, q_ref, *, scale):
    acc = _dot(cq_ref[...], w_ref[...])
    c2 = c2_ref[...]
    for hh in range(q_ref.shape[0]):
        base = hh * MLA_QK
        q_ref[hh, :, :MLA_NOPE] = (acc[:, base:base + MLA_NOPE] * scale).astype(BF16)
        rope = _rope_pair(acc[:, base + MLA_NOPE:base + MLA_QK], c2)
        q_ref[hh, :, MLA_NOPE:] = (rope * scale).astype(BF16)


def mla_q_proj(cq, w, c2, *, tm, heads_per_step, seq_len, scale):
    n, q_rank = cq.shape
    heads = w.shape[1] // MLA_QK
    tn = heads_per_step * MLA_QK
    pos_blocks = seq_len // tm
    return pl.pallas_call(
        functools.partial(_mla_q_kernel, scale=scale),
        grid=(n // tm, heads // heads_per_step),
        in_specs=[
            pl.BlockSpec((tm, q_rank), lambda i, j: (i, 0)),
            pl.BlockSpec((q_rank, tn), lambda i, j: (0, j)),
            pl.BlockSpec((tm, LANES), lambda i, j: (i % pos_blocks, 0)),
        ],
        out_specs=pl.BlockSpec((heads_per_step, tm, MLA_QK), lambda i, j: (j, i, 0)),
        out_shape=jax.ShapeDtypeStruct((heads, n, MLA_QK), BF16),
        compiler_params=_params("parallel", "parallel"),
        name="mla_q_proj",
    )(cq, w, c2)


def _mla_kv_kernel(ckv_ref, kr_ref, wk_ref, wvt_ref, k_ref, vt_ref):
    ckv = ckv_ref[...]
    kn = _dot(ckv, wk_ref[...])
    vt = _dot_nt(wvt_ref[...], ckv)
    kr = kr_ref[...]
    tm = ckv.shape[0]
    row = lax.broadcasted_iota(jnp.int32, (BF16_SUBLANES, tm), 0)
    ones_rows = jnp.where(row == 0, 1.0, 0.0).astype(BF16)
    for hh in range(k_ref.shape[0]):
        k_ref[hh, :, :MLA_NOPE] = kn[:, hh * MLA_NOPE:(hh + 1) * MLA_NOPE].astype(BF16)
        k_ref[hh, :, MLA_NOPE:] = kr
        vt_ref[hh, :MLA_V, :] = vt[hh * MLA_V:(hh + 1) * MLA_V, :].astype(BF16)
        vt_ref[hh, MLA_V:, :] = ones_rows


def mla_kv_proj(ckv, kr, wk, wvt, *, tm, heads_per_step):
    n, kv_rank = ckv.shape
    heads = wk.shape[1] // MLA_NOPE
    hps = heads_per_step
    return pl.pallas_call(
        _mla_kv_kernel,
        grid=(n // tm, heads // hps),
        in_specs=[
            pl.BlockSpec((tm, kv_rank), lambda i, j: (i, 0)),
            pl.BlockSpec((tm, LANES), lambda i, j: (i, 0)),
            pl.BlockSpec((kv_rank, hps * MLA_NOPE), lambda i, j: (0, j)),
            pl.BlockSpec((hps * MLA_V, kv_rank), lambda i, j: (j, 0)),
        ],
        out_specs=[
            pl.BlockSpec((hps, tm, MLA_QK), lambda i, j: (j, i, 0)),
            pl.BlockSpec((hps, MLA_VT_ROWS, tm), lambda i, j: (j, 0, i)),
        ],
        out_shape=[
            jax.ShapeDtypeStruct((heads, n, MLA_QK), BF16),
            jax.ShapeDtypeStruct((heads, MLA_VT_ROWS, n), BF16),
        ],
        compiler_params=_params("parallel", "parallel"),
        name="mla_kv_proj",
    )(ckv, kr, wk, wvt)


def _mla_attn_kernel(q_ref, k_ref, vt_ref, o_ref, m_ref, acc_ref, s0_ref, s1_ref, *, tk):
    n_kt = k_ref.shape[1] // tk
    assert n_kt % 2 == 0
    m_ref[...] = jnp.full(m_ref.shape, MASK_VALUE, F32)
    acc_ref[...] = jnp.zeros(acc_ref.shape, F32)

    def scores(kt, s_ref):
        start = pl.multiple_of(kt * tk, tk)
        s_ref[...] = _dot_nt(k_ref[0, pl.ds(start, tk), :], q_ref[0])

    def softmax_pv(kt, s_ref):
        start = pl.multiple_of(kt * tk, tk)
        vt = vt_ref[0, :, pl.ds(start, tk)]
        m_prev = m_ref[...]
        m_new = jnp.maximum(m_prev, jnp.max(s_ref[...], axis=0, keepdims=True))
        p = jnp.exp2(s_ref[...] - m_new).astype(BF16)
        alpha = jnp.exp2(m_prev - m_new)
        acc_ref[...] = acc_ref[...] * alpha + _dot(vt, p)
        m_ref[...] = m_new

    scores(0, s0_ref)

    def body(j, carry):
        kt = 2 * j
        scores(kt + 1, s1_ref)
        softmax_pv(kt, s0_ref)
        scores(kt + 2, s0_ref)
        softmax_pv(kt + 1, s1_ref)
        return carry

    lax.fori_loop(0, n_kt // 2 - 1, body, 0)
    scores(n_kt - 1, s1_ref)
    softmax_pv(n_kt - 2, s0_ref)
    softmax_pv(n_kt - 1, s1_ref)
    acc = acc_ref[...]
    o_t = acc[:MLA_V, :] / acc[MLA_V:MLA_V + 1, :]
    o_ref[0] = o_t.T.astype(BF16)


def mla_attention(q_hm, k_hm, vt_hm, *, n_seq, seq_len, tq, tk):
    heads, n, _ = q_hm.shape
    nq = seq_len // tq
    return pl.pallas_call(
        functools.partial(_mla_attn_kernel, tk=tk),
        grid=(n_seq, heads, nq),
        in_specs=[
            pl.BlockSpec((1, tq, MLA_QK), lambda s, h, i: (h, s * nq + i, 0)),
            pl.BlockSpec((1, seq_len, MLA_QK), lambda s, h, i: (h, s, 0)),
            pl.BlockSpec((1, MLA_VT_ROWS, seq_len), lambda s, h, i: (h, 0, s)),
        ],
        out_specs=pl.BlockSpec((1, tq, MLA_V), lambda s, h, i: (h, s * nq + i, 0)),
        out_shape=jax.ShapeDtypeStruct((heads, n, MLA_V), BF16),
        scratch_shapes=[pltpu.VMEM((1, tq), F32), pltpu.VMEM((MLA_VT_ROWS, tq), F32),
                        pltpu.VMEM((tk, tq), F32), pltpu.VMEM((tk, tq), F32)],
        compiler_params=_params("parallel", "parallel", "arbitrary"),
        name="mla_attention",
    )(q_hm, k_hm, vt_hm)


def _norm_kernel(x_ref, g_ref, y_ref):
    y_ref[...] = _rmsnorm_f32(x_ref[...], g_ref[...])


def final_norm(x, g, *, tm):
    n, d = x.shape
    return pl.pallas_call(
        _norm_kernel,
        grid=(n // tm,),
        in_specs=[pl.BlockSpec((tm, d), lambda i: (i, 0)), pl.BlockSpec((1, d), lambda i: (0, 0))],
        out_specs=pl.BlockSpec((tm, d), lambda i: (i, 0)),
        out_shape=jax.ShapeDtypeStruct((n, d), F32),
        compiler_params=_params("parallel"),
        name="final_norm",
    )(x, g)


def rope_table(seq_len):
    inv_freq = ROPE_THETA ** (-jnp.arange(0, MLA_ROPE, 2, dtype=F32) / MLA_ROPE)
    ang = jnp.arange(seq_len, dtype=F32)[:, None] * inv_freq[None, :]
    cos, sin = jnp.cos(ang), jnp.sin(ang)
    return jnp.concatenate([cos, cos, -sin, sin], axis=-1)


def _swap_halves(w):
    half = w.shape[-1] // 2
    return jnp.concatenate([w[..., half:], w[..., :half]], axis=-1)


def trunk(x, n_seq, seq_len, p, tiles):
    n, d = x.shape
    rows = seq_len // GRID_W
    row2 = lambda v: v.reshape(1, -1)

    w_qkv = p["na_w_qkv"][0].astype(BF16)
    na_dh = d // NA_HEADS
    col_scale = jnp.concatenate([jnp.full((d,), na_dh ** -0.5, F32), jnp.ones((2 * d,), F32)])
    qkv = qkv_proj(x, row2(p["g_mix"][0]), w_qkv, row2(p["na_b_qkv"][0]), row2(col_scale),
                   tm=tiles["qkv_tm"], tn=tiles["qkv_tn"])
    bias = na_bias_table(p["na_rpb"][0], rows)
    o = na_attention(qkv, bias, n_seq=n_seq, seq_len=seq_len)
    x = out_proj(o, p["na_w_o"][0].astype(BF16), x, tm=tiles["tm"])
    x = _ffn(x, p, 0, tiles)

    w_kr = p["mla_w_kr"][0]
    w_down = jnp.concatenate([p["mla_w_dq"][0], p["mla_w_dkv"][0], w_kr, _swap_halves(w_kr)], axis=1).astype(BF16)
    c2 = rope_table(seq_len)
    cq, ckv, kr = mla_down(x, row2(p["g_mix"][1]), w_down, row2(p["mla_g_q"][0]), row2(p["mla_g_kv"][0]), c2,
                           tm=tiles["tm"], seq_len=seq_len)
    q_rank = cq.shape[1]
    w_uq = p["mla_w_uq"][0].reshape(q_rank, MLA_HEADS, MLA_NOPE + MLA_ROPE)
    w_uq = jnp.concatenate([w_uq, _swap_halves(w_uq[..., MLA_NOPE:])], axis=-1)
    w_uq = w_uq.reshape(q_rank, MLA_HEADS * MLA_QK).astype(BF16)
    q_hm = mla_q_proj(cq, w_uq, c2, tm=tiles["tm"], heads_per_step=tiles["q_hps"], seq_len=seq_len,
                      scale=LOG2_E * (MLA_NOPE + MLA_ROPE) ** -0.5)
    kv_rank = ckv.shape[1]
    w_ukv = p["mla_w_ukv"][0].reshape(kv_rank, MLA_HEADS, MLA_NOPE + MLA_V)
    w_uk = w_ukv[..., :MLA_NOPE].reshape(kv_rank, MLA_HEADS * MLA_NOPE).astype(BF16)
    w_uvt = w_ukv[..., MLA_NOPE:].reshape(kv_rank, MLA_HEADS * MLA_V).T.astype(BF16)
    k_hm, vt_hm = mla_kv_proj(ckv, kr, w_uk, w_uvt, tm=tiles["tm"], heads_per_step=tiles["kv_hps"])
    o = mla_attention(q_hm, k_hm, vt_hm, n_seq=n_seq, seq_len=seq_len, tq=tiles["mla_tq"], tk=tiles["mla_tk"])
    x = out_proj(o, p["mla_w_o"][0].astype(BF16), x, tm=tiles["tm"])
    x = _ffn(x, p, 1, tiles)
    return final_norm(x, row2(p["g_final"]), tm=tiles["tm"])


def _ffn(x, p, layer, tiles):
    a = ffn_up(x, p["g_ffn"][layer].reshape(1, -1), p["ffn_w_gate"][layer].astype(BF16),
               p["ffn_w_up"][layer].astype(BF16), tm=tiles["ffn_tm"], tn=tiles["ffn_tn"])
    return ffn_down(a, p["ffn_w_down"][layer].astype(BF16), x, tm=tiles["tm"], tn=tiles["down_tn"])


TILES = dict(tm=512, qkv_tm=1024, qkv_tn=1024, ffn_tm=1024, ffn_tn=512, down_tn=1024, q_hps=4, kv_hps=8,
             mla_tq=1024, mla_tk=1024)


def kernel(x_prompt, x_sample, g_mix, g_ffn, g_final, na_w_qkv, na_b_qkv, na_rpb, na_w_o, mla_w_dq, mla_g_q,
           mla_w_uq, mla_w_dkv, mla_g_kv, mla_w_kr, mla_w_ukv, mla_w_o, ffn_w_gate, ffn_w_up, ffn_w_down):
    p = dict(g_mix=g_mix, g_ffn=g_ffn, g_final=g_final, na_w_qkv=na_w_qkv, na_b_qkv=na_b_qkv, na_rpb=na_rpb,
             na_w_o=na_w_o, mla_w_dq=mla_w_dq, mla_g_q=mla_g_q, mla_w_uq=mla_w_uq, mla_w_dkv=mla_w_dkv,
             mla_g_kv=mla_g_kv, mla_w_kr=mla_w_kr, mla_w_ukv=mla_w_ukv, mla_w_o=mla_w_o,
             ffn_w_gate=ffn_w_gate, ffn_w_up=ffn_w_up, ffn_w_down=ffn_w_down)
    bp, seq_len, d = x_prompt.shape
    bs = x_sample.shape[0]
    assert x_sample.shape[1:] == (seq_len, d)
    x = jnp.concatenate([x_prompt.reshape(bp * seq_len, d), x_sample.reshape(bs * seq_len, d)], axis=0)
    y = trunk(x, bp + bs, seq_len, p, TILES)
    y = y.reshape(bp + bs, seq_len, d)
    return (y[:bp], y[bp:])
```

```python
import functools

import jax
import jax.numpy as jnp
from jax import lax
from jax.experimental import pallas as pl
from jax.experimental.pallas import tpu as pltpu

GRID_W = 64
NA_HEADS = 16
NA_KR = 8
NA_KC = 16
MLA_HEADS = 16
MLA_NOPE = 128
MLA_ROPE = 64
MLA_V = 128
ROPE_THETA = 10000.0
EPS = 1e-6
LOG2_E = 1.4426950408889634

LANES = 128
BF16_SUBLANES = 16
VMEM_LIMIT_BYTES = 56 * 1024 * 1024

NA_QROWS = 8
NA_KROWS = 16
NA_KBLK_ROWS = 4
MASK_VALUE = -1e30

MLA_QK = 256
MLA_VT_ROWS = MLA_V + BF16_SUBLANES

F32 = jnp.float32
BF16 = jnp.bfloat16


def _params(*sem):
    return pltpu.CompilerParams(dimension_semantics=sem, vmem_limit_bytes=VMEM_LIMIT_BYTES)


def _rmsnorm_f32(x, g):
    return x * lax.rsqrt(jnp.mean(x * x, axis=-1, keepdims=True) + EPS) * g


def _dot(a, b):
    return jnp.dot(a, b, preferred_element_type=F32)


def _dot_nt(a, b):
    return lax.dot_general(a, b, (((1,), (1,)), ((), ())), preferred_element_type=F32)


def _qkv_kernel(x_ref, g_ref, w_ref, b_ref, cs_ref, o_ref, h_ref):
    @pl.when(pl.program_id(1) == 0)
    def _():
        h_ref[...] = _rmsnorm_f32(x_ref[...], g_ref[...]).astype(BF16)

    acc = (_dot(h_ref[...], w_ref[...]) + b_ref[...]) * cs_ref[...]
    for hh in range(o_ref.shape[0]):
        o_ref[hh] = acc[:, hh * LANES:(hh + 1) * LANES].astype(BF16)


def qkv_proj(x, g, w, b, cs, *, tm, tn):
    n, d = x.shape
    m = w.shape[1]
    return pl.pallas_call(
        _qkv_kernel,
        grid=(n // tm, m // tn),
        in_specs=[
            pl.BlockSpec((tm, d), lambda i, j: (i, 0)),
            pl.BlockSpec((1, d), lambda i, j: (0, 0)),
            pl.BlockSpec((d, tn), lambda i, j: (0, j)),
            pl.BlockSpec((1, tn), lambda i, j: (0, j)),
            pl.BlockSpec((1, tn), lambda i, j: (0, j)),
        ],
        out_specs=pl.BlockSpec((tn // LANES, tm, LANES), lambda i, j: (j, i, 0)),
        out_shape=jax.ShapeDtypeStruct((m // LANES, n, LANES), BF16),
        scratch_shapes=[pltpu.VMEM((tm, d), BF16)],
        compiler_params=_params("parallel", "arbitrary"),
        name="qkv_proj",
    )(x, g, w, b, cs)


def _na_kernel(q_ref, k0, k1, k2, k3, v0, v1, v2, v3, bias_ref, o_ref):
    k = jnp.concatenate([k0[0], k1[0], k2[0], k3[0]], axis=0)
    v = jnp.concatenate([v0[0], v1[0], v2[0], v3[0]], axis=0)
    half = q_ref.shape[1] // 2
    for c in range(2):
        rows = pl.ds(c * half, half)
        s = _dot_nt(q_ref[0, rows, :], k) + bias_ref[0, 0, rows, :]
        m = jnp.max(s, axis=-1, keepdims=True)
        p = jnp.exp(s - m)
        l = jnp.sum(p, axis=-1, keepdims=True)
        o = _dot(p.astype(BF16), v)
        o_ref[0, rows, :] = (o / l).astype(BF16)


def na_bias_table(rpb, rows):
    heads = rpb.shape[0]
    nb = rows // NA_QROWS
    cols = jnp.arange(GRID_W)
    c_start = jnp.clip(cols - NA_KC // 2, 0, GRID_W - NA_KC)
    cvalid = (cols[None, :] >= c_start[:, None]) & (cols[None, :] < c_start[:, None] + NA_KC)
    coff = jnp.clip(cols[None, :] - cols[:, None] + (NA_KC - 1), 0, 2 * NA_KC - 2)
    a = rpb[:, :, coff]
    tables = []
    for b in (0, 1, nb - 1):
        r = b * NA_QROWS + jnp.arange(NA_QROWS)
        r_start = jnp.clip(r - NA_KR // 2, 0, rows - NA_KR)
        kb = min(max(b * NA_QROWS - NA_KR // 2, 0), rows - NA_KROWS)
        krow = kb + jnp.arange(NA_KROWS)
        rvalid = (krow[None, :] >= r_start[:, None]) & (krow[None, :] < r_start[:, None] + NA_KR)
        roff = jnp.clip(krow[None, :] - r[:, None] + (NA_KR - 1), 0, 2 * NA_KR - 2)
        t = a[:, roff]
        valid = rvalid[:, :, None, None] & cvalid[None, None]
        t = jnp.where(valid[None], t, MASK_VALUE)
        t = t.transpose(0, 1, 3, 2, 4).reshape(heads, NA_QROWS * GRID_W, NA_KROWS * GRID_W)
        tables.append(t)
    return jnp.stack(tables).astype(F32)


def na_attention(qkv_hm, bias, *, n_seq, seq_len):
    heads = NA_HEADS
    rows = seq_len // GRID_W
    assert rows % NA_QROWS == 0 and rows >= 3 * NA_QROWS
    nb = rows // NA_QROWS
    tq = NA_QROWS * GRID_W
    tkb = NA_KBLK_ROWS * GRID_W
    n_kblk = NA_KROWS // NA_KBLK_ROWS
    kblk_per_seq = seq_len // tkb
    n = n_seq * seq_len

    def q_map(h, s, b):
        return (h, s * nb + b, 0)

    def kv_map(which, j):
        def f(h, s, b):
            base = jnp.clip(2 * b - 1, 0, kblk_per_seq - n_kblk)
            return (which * heads + h, s * kblk_per_seq + base + j, 0)
        return f

    def bias_map(h, s, b):
        cls = jnp.where(b == 0, 0, jnp.where(b == nb - 1, 2, 1))
        return (cls, h, 0, 0)

    in_specs = [pl.BlockSpec((1, tq, LANES), q_map)]
    in_specs += [pl.BlockSpec((1, tkb, LANES), kv_map(1, j)) for j in range(n_kblk)]
    in_specs += [pl.BlockSpec((1, tkb, LANES), kv_map(2, j)) for j in range(n_kblk)]
    in_specs += [pl.BlockSpec((1, 1, tq, NA_KROWS * GRID_W), bias_map)]
    return pl.pallas_call(
        _na_kernel,
        grid=(heads, n_seq, nb),
        in_specs=in_specs,
        out_specs=pl.BlockSpec((1, tq, LANES), q_map),
        out_shape=jax.ShapeDtypeStruct((heads, n, LANES), BF16),
        compiler_params=_params("parallel", "parallel", "arbitrary"),
        name="na_attention",
    )(*([qkv_hm] * (1 + 2 * n_kblk)), bias)


def _resident(block_shape, index_map):
    return pl.BlockSpec(block_shape, index_map, pipeline_mode=pl.Buffered(1))


def _oproj_kernel(o_ref, w_ref, x_ref, g_ref, y_ref, h_ref):
    o = jnp.concatenate([o_ref[h] for h in range(o_ref.shape[0])], axis=-1)
    y = x_ref[...] + _dot(o, w_ref[...])
    y_ref[...] = y
    h_ref[...] = _rmsnorm_f32(y, g_ref[...]).astype(BF16)


def out_proj(o_hm, w, x, g, *, tm):
    heads, n, dh = o_hm.shape
    d = w.shape[1]
    return pl.pallas_call(
        _oproj_kernel,
        grid=(n // tm,),
        in_specs=[
            pl.BlockSpec((heads, tm, dh), lambda i: (0, i, 0)),
            _resident((heads * dh, d), lambda i: (0, 0)),
            pl.BlockSpec((tm, d), lambda i: (i, 0)),
            pl.BlockSpec((1, d), lambda i: (0, 0)),
        ],
        out_specs=[pl.BlockSpec((tm, d), lambda i: (i, 0)), pl.BlockSpec((tm, d), lambda i: (i, 0))],
        out_shape=[jax.ShapeDtypeStruct((n, d), F32), jax.ShapeDtypeStruct((n, d), BF16)],
        compiler_params=_params("parallel"),
        name="out_proj",
    )(o_hm, w, x, g)


def _ffn_up_kernel(h_ref, wg_ref, wu_ref, a_ref):
    h = h_ref[...]
    gate = _dot(h, wg_ref[...])
    up = _dot(h, wu_ref[...])
    a_ref[...] = (gate * jax.nn.sigmoid(gate) * up).astype(BF16)


def ffn_up(h, wg, wu, *, tm, tn):
    n, d = h.shape
    f = wg.shape[1]
    return pl.pallas_call(
        _ffn_up_kernel,
        grid=(n // tm, f // tn),
        in_specs=[
            pl.BlockSpec((tm, d), lambda i, j: (i, 0)),
            pl.BlockSpec((d, tn), lambda i, j: (0, j)),
            pl.BlockSpec((d, tn), lambda i, j: (0, j)),
        ],
        out_specs=pl.BlockSpec((tm, tn), lambda i, j: (i, j)),
        out_shape=jax.ShapeDtypeStruct((n, f), BF16),
        compiler_params=_params("parallel", "parallel"),
        name="ffn_up",
    )(h, wg, wu)


def _ffn_down_kernel(a_ref, w_ref, x_ref, g_ref, y_ref, h_ref):
    y = x_ref[...] + _dot(a_ref[...], w_ref[...])
    y_ref[...] = y
    h_ref[...] = _rmsnorm_f32(y, g_ref[...]).astype(BF16)


def _ffn_down_final_kernel(a_ref, w_ref, x_ref, g_ref, y_ref):
    y_ref[...] = _rmsnorm_f32(x_ref[...] + _dot(a_ref[...], w_ref[...]), g_ref[...])


def ffn_down(a, w, x, g, *, tm, final=False, row_start=0, n_rows=None):
    f = a.shape[1]
    d = w.shape[1]
    n_rows = a.shape[0] if n_rows is None else n_rows
    off = row_start // tm
    assert row_start % tm == 0 and n_rows % tm == 0
    in_specs = [
        pl.BlockSpec((tm, f), lambda i: (i + off, 0)),
        _resident((f, d), lambda i: (0, 0)),
        pl.BlockSpec((tm, d), lambda i: (i + off, 0)),
        pl.BlockSpec((1, d), lambda i: (0, 0)),
    ]
    row_block = pl.BlockSpec((tm, d), lambda i: (i, 0))
    if final:
        body, out_specs, out_shape = _ffn_down_final_kernel, row_block, jax.ShapeDtypeStruct((n_rows, d), F32)
    else:
        body, out_specs = _ffn_down_kernel, [row_block, row_block]
        out_shape = [jax.ShapeDtypeStruct((n_rows, d), F32), jax.ShapeDtypeStruct((n_rows, d), BF16)]
    return pl.pallas_call(
        body,
        grid=(n_rows // tm,),
        in_specs=in_specs,
        out_specs=out_specs,
        out_shape=out_shape,
        compiler_params=_params("parallel"),
        name="ffn_down_final" if final else "ffn_down",
    )(a, w, x, g)


def _rope_pair(g128, c2):
    prod = g128 * c2
    both = prod + pltpu.roll(prod, LANES // 2, axis=1)
    lane = lax.broadcasted_iota(jnp.int32, both.shape, 1)
    return jnp.where(lane < MLA_ROPE, both, 0.0)


def _mla_down_kernel(h_ref, w_ref, gq_ref, gkv_ref, c2_ref, cq_ref, ckv_ref, kr_ref, *, q_rank, kv_rank):
    c = _dot(h_ref[...], w_ref[...])
    cq_ref[...] = _rmsnorm_f32(c[:, :q_rank], gq_ref[...]).astype(BF16)
    ckv_ref[...] = _rmsnorm_f32(c[:, q_rank:q_rank + kv_rank], gkv_ref[...]).astype(BF16)
    kr_ref[...] = _rope_pair(c[:, q_rank + kv_rank:], c2_ref[...]).astype(BF16)


def mla_down(h, w, gq, gkv, c2, *, tm, seq_len):
    n, d = h.shape
    q_rank, kv_rank = gq.shape[1], gkv.shape[1]
    width = w.shape[1]
    assert width == q_rank + kv_rank + LANES
    pos_blocks = seq_len // tm
    return pl.pallas_call(
        functools.partial(_mla_down_kernel, q_rank=q_rank, kv_rank=kv_rank),
        grid=(n // tm,),
        in_specs=[
            pl.BlockSpec((tm, d), lambda i: (i, 0)),
            _resident((d, width), lambda i: (0, 0)),
            pl.BlockSpec((1, q_rank), lambda i: (0, 0)),
            pl.BlockSpec((1, kv_rank), lambda i: (0, 0)),
            pl.BlockSpec((tm, LANES), lambda i: (i % pos_blocks, 0)),
        ],
        out_specs=[
            pl.BlockSpec((tm, q_rank), lambda i: (i, 0)),
            pl.BlockSpec((tm, kv_rank), lambda i: (i, 0)),
            pl.BlockSpec((tm, LANES), lambda i: (i, 0)),
        ],
        out_shape=[
            jax.ShapeDtypeStruct((n, q_rank), BF16),
            jax.ShapeDtypeStruct((n, kv_rank), BF16),
            jax.ShapeDtypeStruct((n, LANES), BF16),
        ],
        compiler_params=_params("parallel"),
        name="mla_down",
    )(h, w, gq, gkv, c2)


def _mla_q_kernel(cq_ref, w_ref, c2_ref, q_ref, *, scale):
    acc = _dot(cq_ref[...], w_ref[...])
    c2 = c2_ref[...]
    for hh in range(q_ref.shape[0]):
        base = hh * MLA_QK
        q_ref[hh, :, :MLA_NOPE] = (acc[:, base:base + MLA_NOPE] * scale).astype(BF16)
        rope = _rope_pair(acc[:, base + MLA_NOPE:base + MLA_QK], c2)
        q_ref[hh, :, MLA_NOPE:] = (rope * scale).astype(BF16)


def mla_q_proj(cq, w, c2, *, tm, heads_per_step, seq_len, scale):
    n, q_rank = cq.shape
    heads = w.shape[1] // MLA_QK
    tn = heads_per_step * MLA_QK
    pos_blocks = seq_len // tm
    return pl.pallas_call(
        functools.partial(_mla_q_kernel, scale=scale),
        grid=(n // tm, heads // heads_per_step),
        in_specs=[
            pl.BlockSpec((tm, q_rank), lambda i, j: (i, 0)),
            pl.BlockSpec((q_rank, tn), lambda i, j: (0, j)),
            pl.BlockSpec((tm, LANES), lambda i, j: (i % pos_blocks, 0)),
        ],
        out_specs=pl.BlockSpec((heads_per_step, tm, MLA_QK), lambda i, j: (j, i, 0)),
        out_shape=jax.ShapeDtypeStruct((heads, n, MLA_QK), BF16),
        compiler_params=_params("parallel", "parallel"),
        name="mla_q_proj",
    )(cq, w, c2)


def _mla_kv_kernel(ckv_ref, kr_ref, wk_ref, wvt_ref, k_ref, vt_ref):
    ckv = ckv_ref[...]
    kn = _dot(ckv, wk_ref[...])
    vt = _dot_nt(wvt_ref[...], ckv)
    kr = kr_ref[...]
    tm = ckv.shape[0]
    row = lax.broadcasted_iota(jnp.int32, (BF16_SUBLANES, tm), 0)
    ones_rows = jnp.where(row == 0, 1.0, 0.0).astype(BF16)
    for hh in range(k_ref.shape[0]):
        k_ref[hh, :, :MLA_NOPE] = kn[:, hh * MLA_NOPE:(hh + 1) * MLA_NOPE].astype(BF16)
        k_ref[hh, :, MLA_NOPE:] = kr
        vt_ref[hh, :MLA_V, :] = vt[hh * MLA_V:(hh + 1) * MLA_V, :].astype(BF16)
        vt_ref[hh, MLA_V:, :] = ones_rows


def mla_kv_proj(ckv, kr, wk, wvt, *, tm, heads_per_step):
    n, kv_rank = ckv.shape
    heads = wk.shape[1] // MLA_NOPE
    hps = heads_per_step
    return pl.pallas_call(
        _mla_kv_kernel,
        grid=(n // tm, heads // hps),
        in_specs=[
            pl.BlockSpec((tm, kv_rank), lambda i, j: (i, 0)),
            pl.BlockSpec((tm, LANES), lambda i, j: (i, 0)),
            pl.BlockSpec((kv_rank, hps * MLA_NOPE), lambda i, j: (0, j)),
            pl.BlockSpec((hps * MLA_V, kv_rank), lambda i, j: (j, 0)),
        ],
        out_specs=[
            pl.BlockSpec((hps, tm, MLA_QK), lambda i, j: (j, i, 0)),
            pl.BlockSpec((hps, MLA_VT_ROWS, tm), lambda i, j: (j, 0, i)),
        ],
        out_shape=[
            jax.ShapeDtypeStruct((heads, n, MLA_QK), BF16),
            jax.ShapeDtypeStruct((heads, MLA_VT_ROWS, n), BF16),
        ],
        compiler_params=_params("parallel", "parallel"),
        name="mla_kv_proj",
    )(ckv, kr, wk, wvt)


def _mla_attn_kernel(q_ref, k_ref, vt_ref, o_ref, m_ref, acc_ref, s0_ref, s1_ref, mt0_ref, mt1_ref, *, tk, unroll):
    n_kt = k_ref.shape[1] // tk
    assert unroll % 2 == 0
    s_refs, mt_refs = (s0_ref, s1_ref), (mt0_ref, mt1_ref)
    m_ref[...] = jnp.full(m_ref.shape, MASK_VALUE, F32)
    acc_ref[...] = jnp.zeros(acc_ref.shape, F32)

    def scores(kt, slot):
        start = pl.multiple_of(kt * tk, tk)
        s = _dot_nt(k_ref[0, pl.ds(start, tk), :], q_ref[0])
        s_refs[slot][...] = s
        mt_refs[slot][...] = jnp.max(s, axis=0, keepdims=True)

    def softmax_pv(kt, slot):
        start = pl.multiple_of(kt * tk, tk)
        vt = vt_ref[0, :, pl.ds(start, tk)]
        m_prev = m_ref[...]
        m_new = jnp.maximum(m_prev, mt_refs[slot][...])
        p = jnp.exp2(s_refs[slot][...] - m_new).astype(BF16)
        alpha = jnp.exp2(m_prev - m_new)
        acc_ref[...] = acc_ref[...] * alpha + _dot(vt, p)
        m_ref[...] = m_new

    scores(0, 0)
    n_loop = (n_kt - 1) // unroll

    def body(j, carry):
        for u in range(unroll):
            kt = unroll * j + u
            scores(kt + 1, (u + 1) % 2)
            softmax_pv(kt, u % 2)
        return carry

    lax.fori_loop(0, n_loop, body, 0)
    for kt in range(n_loop * unroll, n_kt):
        if kt + 1 < n_kt:
            scores(kt + 1, (kt + 1) % 2)
        softmax_pv(kt, kt % 2)
    acc = acc_ref[...]
    o_t = acc[:MLA_V, :] / acc[MLA_V:MLA_V + 1, :]
    o_ref[0] = o_t.T.astype(BF16)


def mla_attention(q_hm, k_hm, vt_hm, *, n_seq, seq_len, tq, tk, unroll):
    heads, n, _ = q_hm.shape
    nq = seq_len // tq
    return pl.pallas_call(
        functools.partial(_mla_attn_kernel, tk=tk, unroll=unroll),
        grid=(n_seq, heads, nq),
        in_specs=[
            pl.BlockSpec((1, tq, MLA_QK), lambda s, h, i: (h, s * nq + i, 0)),
            pl.BlockSpec((1, seq_len, MLA_QK), lambda s, h, i: (h, s, 0)),
            pl.BlockSpec((1, MLA_VT_ROWS, seq_len), lambda s, h, i: (h, 0, s)),
        ],
        out_specs=pl.BlockSpec((1, tq, MLA_V), lambda s, h, i: (h, s * nq + i, 0)),
        out_shape=jax.ShapeDtypeStruct((heads, n, MLA_V), BF16),
        scratch_shapes=[pltpu.VMEM((1, tq), F32), pltpu.VMEM((MLA_VT_ROWS, tq), F32),
                        pltpu.VMEM((tk, tq), F32), pltpu.VMEM((tk, tq), F32),
                        pltpu.VMEM((1, tq), F32), pltpu.VMEM((1, tq), F32)],
        compiler_params=_params("parallel", "parallel", "arbitrary"),
        name="mla_attention",
    )(q_hm, k_hm, vt_hm)


def rope_table(seq_len):
    inv_freq = ROPE_THETA ** (-jnp.arange(0, MLA_ROPE, 2, dtype=F32) / MLA_ROPE)
    ang = jnp.arange(seq_len, dtype=F32)[:, None] * inv_freq[None, :]
    cos, sin = jnp.cos(ang), jnp.sin(ang)
    return jnp.concatenate([cos, cos, -sin, sin], axis=-1)


def _swap_halves(w):
    half = w.shape[-1] // 2
    return jnp.concatenate([w[..., half:], w[..., :half]], axis=-1)


def trunk(x, seq_len, out_rows, p, tiles):
    n, d = x.shape
    n_seq = n // seq_len
    rows = seq_len // GRID_W
    row2 = lambda v: v.reshape(1, -1)

    w_qkv = p["na_w_qkv"][0].astype(BF16)
    na_dh = d // NA_HEADS
    col_scale = jnp.concatenate([jnp.full((d,), na_dh ** -0.5, F32), jnp.ones((2 * d,), F32)])
    qkv = qkv_proj(x, row2(p["g_mix"][0]), w_qkv, row2(p["na_b_qkv"][0]), row2(col_scale),
                   tm=tiles["qkv_tm"], tn=tiles["qkv_tn"])
    bias = na_bias_table(p["na_rpb"][0], rows)
    o = na_attention(qkv, bias, n_seq=n_seq, seq_len=seq_len)
    x, h = out_proj(o, p["na_w_o"][0].astype(BF16), x, row2(p["g_ffn"][0]), tm=tiles["tm"])
    a = _ffn_up(h, p, 0, tiles)
    x, h = ffn_down(a, p["ffn_w_down"][0].astype(BF16), x, row2(p["g_mix"][1]), tm=tiles["down_tm"])

    w_kr = p["mla_w_kr"][0]
    w_down = jnp.concatenate([p["mla_w_dq"][0], p["mla_w_dkv"][0], w_kr, _swap_halves(w_kr)], axis=1).astype(BF16)
    c2 = rope_table(seq_len)
    cq, ckv, kr = mla_down(h, w_down, row2(p["mla_g_q"][0]), row2(p["mla_g_kv"][0]), c2,
                           tm=tiles["tm"], seq_len=seq_len)
    q_rank = cq.shape[1]
    w_uq = p["mla_w_uq"][0].reshape(q_rank, MLA_HEADS, MLA_NOPE + MLA_ROPE)
    w_uq = jnp.concatenate([w_uq, _swap_halves(w_uq[..., MLA_NOPE:])], axis=-1)
    w_uq = w_uq.reshape(q_rank, MLA_HEADS * MLA_QK).astype(BF16)
    q_hm = mla_q_proj(cq, w_uq, c2, tm=tiles["tm"], heads_per_step=tiles["q_hps"], seq_len=seq_len,
                      scale=LOG2_E * (MLA_NOPE + MLA_ROPE) ** -0.5)
    kv_rank = ckv.shape[1]
    w_ukv = p["mla_w_ukv"][0].reshape(kv_rank, MLA_HEADS, MLA_NOPE + MLA_V)
    w_uk = w_ukv[..., :MLA_NOPE].reshape(kv_rank, MLA_HEADS * MLA_NOPE).astype(BF16)
    w_uvt = w_ukv[..., MLA_NOPE:].reshape(kv_rank, MLA_HEADS * MLA_V).T.astype(BF16)
    k_hm, vt_hm = mla_kv_proj(ckv, kr, w_uk, w_uvt, tm=tiles["tm"], heads_per_step=tiles["kv_hps"])
    o = mla_attention(q_hm, k_hm, vt_hm, n_seq=n_seq, seq_len=seq_len, tq=tiles["mla_tq"], tk=tiles["mla_tk"],
                      unroll=tiles["mla_unroll"])
    x, h = out_proj(o, p["mla_w_o"][0].astype(BF16), x, row2(p["g_ffn"][1]), tm=tiles["tm"])
    a = _ffn_up(h, p, 1, tiles)
    w_down = p["ffn_w_down"][1].astype(BF16)
    return [ffn_down(a, w_down, x, row2(p["g_final"]), tm=tiles["down_tm"], final=True, row_start=r0, n_rows=nr)
            for r0, nr in out_rows]


def _ffn_up(h, p, layer, tiles):
    return ffn_up(h, p["ffn_w_gate"][layer].astype(BF16), p["ffn_w_up"][layer].astype(BF16),
                  tm=tiles["ffn_tm"], tn=tiles["ffn_tn"])


TILES = dict(tm=512, qkv_tm=1024, qkv_tn=1024, ffn_tm=1024, ffn_tn=512, down_tm=256, q_hps=4, kv_hps=8,
             mla_tq=1024, mla_tk=1024, mla_unroll=4)


def kernel(x_prompt, x_sample, g_mix, g_ffn, g_final, na_w_qkv, na_b_qkv, na_rpb, na_w_o, mla_w_dq, mla_g_q,
           mla_w_uq, mla_w_dkv, mla_g_kv, mla_w_kr, mla_w_ukv, mla_w_o, ffn_w_gate, ffn_w_up, ffn_w_down):
    p = dict(g_mix=g_mix, g_ffn=g_ffn, g_final=g_final, na_w_qkv=na_w_qkv, na_b_qkv=na_b_qkv, na_rpb=na_rpb,
             na_w_o=na_w_o, mla_w_dq=mla_w_dq, mla_g_q=mla_g_q, mla_w_uq=mla_w_uq, mla_w_dkv=mla_w_dkv,
             mla_g_kv=mla_g_kv, mla_w_kr=mla_w_kr, mla_w_ukv=mla_w_ukv, mla_w_o=mla_w_o,
             ffn_w_gate=ffn_w_gate, ffn_w_up=ffn_w_up, ffn_w_down=ffn_w_down)
    bp, seq_len, d = x_prompt.shape
    bs = x_sample.shape[0]
    assert x_sample.shape[1:] == (seq_len, d)
    x = jnp.concatenate([x_prompt.reshape(bp * seq_len, d), x_sample.reshape(bs * seq_len, d)], axis=0)
    out_rows = [(0, bp * seq_len), (bp * seq_len, bs * seq_len)]
    y_prompt, y_sample = trunk(x, seq_len, out_rows, p, TILES)
    return (y_prompt.reshape(bp, seq_len, d), y_sample.reshape(bs, seq_len, d))
```

```python
import functools

import jax
import jax.numpy as jnp
from jax import lax
from jax.experimental import pallas as pl
from jax.experimental.pallas import tpu as pltpu

GRID_W = 64
NA_HEADS = 16
NA_KR = 8
NA_KC = 16
MLA_HEADS = 16
MLA_NOPE = 128
MLA_ROPE = 64
MLA_V = 128
ROPE_THETA = 10000.0
EPS = 1e-6
LOG2_E = 1.4426950408889634

LANES = 128
BF16_SUBLANES = 16
VMEM_LIMIT_BYTES = 56 * 1024 * 1024

NA_QROWS = 4
NA_KROWS = 12
NA_KBLK_ROWS = 4
NA_UNITS = 8
NA_WINDOW_BLOCKS = NA_UNITS + NA_KROWS // NA_KBLK_ROWS - 1
MASK_VALUE = -1e30

MLA_QK = 256
MLA_VT_ROWS = MLA_V + BF16_SUBLANES

F32 = jnp.float32
BF16 = jnp.bfloat16


def _params(*sem):
    return pltpu.CompilerParams(dimension_semantics=sem, vmem_limit_bytes=VMEM_LIMIT_BYTES)


def _rmsnorm_f32(x, g):
    return x * lax.rsqrt(jnp.mean(x * x, axis=-1, keepdims=True) + EPS) * g


def _dot(a, b):
    return jnp.dot(a, b, preferred_element_type=F32)


def _dot_nt(a, b):
    return lax.dot_general(a, b, (((1,), (1,)), ((), ())), preferred_element_type=F32)


def _qkv_kernel(x_ref, g_ref, w_ref, b_ref, cs_ref, o_ref, h_ref):
    @pl.when(pl.program_id(1) == 0)
    def _():
        h_ref[...] = _rmsnorm_f32(x_ref[...], g_ref[...]).astype(BF16)

    acc = (_dot(h_ref[...], w_ref[...]) + b_ref[...]) * cs_ref[...]
    for hh in range(o_ref.shape[0]):
        o_ref[hh] = acc[:, hh * LANES:(hh + 1) * LANES].astype(BF16)


def qkv_proj(x, g, w, b, cs, *, tm, tn):
    n, d = x.shape
    m = w.shape[1]
    return pl.pallas_call(
        _qkv_kernel,
        grid=(n // tm, m // tn),
        in_specs=[
            pl.BlockSpec((tm, d), lambda i, j: (i, 0)),
            pl.BlockSpec((1, d), lambda i, j: (0, 0)),
            pl.BlockSpec((d, tn), lambda i, j: (0, j)),
            pl.BlockSpec((1, tn), lambda i, j: (0, j)),
            pl.BlockSpec((1, tn), lambda i, j: (0, j)),
        ],
        out_specs=pl.BlockSpec((tn // LANES, tm, LANES), lambda i, j: (j, i, 0)),
        out_shape=jax.ShapeDtypeStruct((m // LANES, n, LANES), BF16),
        scratch_shapes=[pltpu.VMEM((tm, d), BF16)],
        compiler_params=_params("parallel", "arbitrary"),
        name="qkv_proj",
    )(x, g, w, b, cs)


def _na_kernel(q_ref, k_ref, v_ref, bias_ref, o_ref, *, n_units):
    step = pl.program_id(2)
    tq = NA_QROWS * GRID_W
    tk = NA_KROWS * GRID_W
    window_base = _na_window_base(step, n_units)
    for u in range(NA_UNITS):
        unit = NA_UNITS * step + u
        start = pl.multiple_of((_na_key_base(unit, n_units) - window_base) * tq, tq)
        k = k_ref[0, pl.ds(start, tk), :]
        v = v_ref[0, pl.ds(start, tk), :]
        cls = jnp.where(unit == 0, 0, jnp.where(unit == n_units - 1, 2, 1))
        rows = pl.ds(u * tq, tq)
        s = _dot_nt(q_ref[0, rows, :], k) + bias_ref[cls, 0]
        m = jnp.max(s, axis=-1, keepdims=True)
        p = jnp.exp(s - m)
        l = jnp.sum(p, axis=-1, keepdims=True)
        o = _dot(p.astype(BF16), v)
        o_ref[0, rows, :] = (o / l).astype(BF16)


def _na_key_base(unit, n_units):
    return jnp.clip(unit - 1, 0, n_units - NA_KROWS // NA_KBLK_ROWS)


def _na_window_base(step, n_units):
    return jnp.clip(NA_UNITS * step - 1, 0, n_units - NA_WINDOW_BLOCKS)


def na_bias_table(rpb, rows):
    heads = rpb.shape[0]
    nb = rows // NA_QROWS
    cols = jnp.arange(GRID_W)
    c_start = jnp.clip(cols - NA_KC // 2, 0, GRID_W - NA_KC)
    cvalid = (cols[None, :] >= c_start[:, None]) & (cols[None, :] < c_start[:, None] + NA_KC)
    coff = jnp.clip(cols[None, :] - cols[:, None] + (NA_KC - 1), 0, 2 * NA_KC - 2)
    a = rpb[:, :, coff]
    tables = []
    for b in (0, 1, nb - 1):
        r = b * NA_QROWS + jnp.arange(NA_QROWS)
        r_start = jnp.clip(r - NA_KR // 2, 0, rows - NA_KR)
        kb = NA_KBLK_ROWS * min(max(b - 1, 0), nb - NA_KROWS // NA_KBLK_ROWS)
        krow = kb + jnp.arange(NA_KROWS)
        rvalid = (krow[None, :] >= r_start[:, None]) & (krow[None, :] < r_start[:, None] + NA_KR)
        roff = jnp.clip(krow[None, :] - r[:, None] + (NA_KR - 1), 0, 2 * NA_KR - 2)
        t = a[:, roff]
        valid = rvalid[:, :, None, None] & cvalid[None, None]
        t = jnp.where(valid[None], t, MASK_VALUE)
        t = t.transpose(0, 1, 3, 2, 4).reshape(heads, NA_QROWS * GRID_W, NA_KROWS * GRID_W)
        tables.append(t)
    return jnp.stack(tables).astype(F32)


def na_attention(qkv_hm, bias, *, n_seq, seq_len):
    heads = NA_HEADS
    rows = seq_len // GRID_W
    assert NA_QROWS == NA_KBLK_ROWS and rows % (NA_QROWS * NA_UNITS) == 0
    n_units = rows // NA_QROWS
    assert n_units >= NA_WINDOW_BLOCKS
    nb = n_units // NA_UNITS
    tq = NA_QROWS * GRID_W
    n = n_seq * seq_len

    def q_map(h, s, b):
        return (h, s * nb + b, 0)

    def kv_map(which):
        def f(h, s, b):
            return (which * heads + h, (s * n_units + _na_window_base(b, n_units)) * tq, 0)
        return f

    kv_block = (pl.Element(1), pl.Element(NA_WINDOW_BLOCKS * tq), pl.Element(LANES))
    return pl.pallas_call(
        functools.partial(_na_kernel, n_units=n_units),
        grid=(heads, n_seq, nb),
        in_specs=[
            pl.BlockSpec((1, NA_UNITS * tq, LANES), q_map),
            pl.BlockSpec(kv_block, kv_map(1)),
            pl.BlockSpec(kv_block, kv_map(2)),
            pl.BlockSpec((3, 1, tq, NA_KROWS * GRID_W), lambda h, s, b: (0, h, 0, 0)),
        ],
        out_specs=pl.BlockSpec((1, NA_UNITS * tq, LANES), q_map),
        out_shape=jax.ShapeDtypeStruct((heads, n, LANES), BF16),
        compiler_params=_params("parallel", "parallel", "arbitrary"),
        name="na_attention",
    )(qkv_hm, qkv_hm, qkv_hm, bias)


def _resident(block_shape, index_map):
    return pl.BlockSpec(block_shape, index_map, pipeline_mode=pl.Buffered(1))


def _oproj_kernel(o_ref, w_ref, x_ref, g_ref, y_ref, h_ref):
    o = jnp.concatenate([o_ref[h] for h in range(o_ref.shape[0])], axis=-1)
    y = x_ref[...] + _dot(o, w_ref[...])
    y_ref[...] = y
    h_ref[...] = _rmsnorm_f32(y, g_ref[...]).astype(BF16)


def out_proj(o_hm, w, x, g, *, tm):
    heads, n, dh = o_hm.shape
    d = w.shape[1]
    return pl.pallas_call(
        _oproj_kernel,
        grid=(n // tm,),
        in_specs=[
            pl.BlockSpec((heads, tm, dh), lambda i: (0, i, 0)),
            _resident((heads * dh, d), lambda i: (0, 0)),
            pl.BlockSpec((tm, d), lambda i: (i, 0)),
            pl.BlockSpec((1, d), lambda i: (0, 0)),
        ],
        out_specs=[pl.BlockSpec((tm, d), lambda i: (i, 0)), pl.BlockSpec((tm, d), lambda i: (i, 0))],
        out_shape=[jax.ShapeDtypeStruct((n, d), F32), jax.ShapeDtypeStruct((n, d), BF16)],
        compiler_params=_params("parallel"),
        name="out_proj",
    )(o_hm, w, x, g)


def _ffn_up_kernel(h_ref, wg_ref, wu_ref, a_ref):
    h = h_ref[...]
    gate = _dot(h, wg_ref[...])
    up = _dot(h, wu_ref[...])
    a_ref[...] = (gate * jax.nn.sigmoid(gate) * up).astype(BF16)


def ffn_up(h, wg, wu, *, tm, tn):
    n, d = h.shape
    f = wg.shape[1]
    return pl.pallas_call(
        _ffn_up_kernel,
        grid=(n // tm, f // tn),
        in_specs=[
            pl.BlockSpec((tm, d), lambda i, j: (i, 0)),
            pl.BlockSpec((d, tn), lambda i, j: (0, j)),
            pl.BlockSpec((d, tn), lambda i, j: (0, j)),
        ],
        out_specs=pl.BlockSpec((tm, tn), lambda i, j: (i, j)),
        out_shape=jax.ShapeDtypeStruct((n, f), BF16),
        compiler_params=_params("parallel", "parallel"),
        name="ffn_up",
    )(h, wg, wu)


def _ffn_down_kernel(a_ref, w_ref, x_ref, g_ref, y_ref, h_ref):
    y = x_ref[...] + _dot(a_ref[...], w_ref[...])
    y_ref[...] = y
    h_ref[...] = _rmsnorm_f32(y, g_ref[...]).astype(BF16)


def _ffn_down_final_kernel(a_ref, w_ref, x_ref, g_ref, y_ref):
    y_ref[...] = _rmsnorm_f32(x_ref[...] + _dot(a_ref[...], w_ref[...]), g_ref[...])


def ffn_down(a, w, x, g, *, tm, final=False, row_start=0, n_rows=None):
    f = a.shape[1]
    d = w.shape[1]
    n_rows = a.shape[0] if n_rows is None else n_rows
    off = row_start // tm
    assert row_start % tm == 0 and n_rows % tm == 0
    in_specs = [
        pl.BlockSpec((tm, f), lambda i: (i + off, 0)),
        _resident((f, d), lambda i: (0, 0)),
        pl.BlockSpec((tm, d), lambda i: (i + off, 0)),
        pl.BlockSpec((1, d), lambda i: (0, 0)),
    ]
    row_block = pl.BlockSpec((tm, d), lambda i: (i, 0))
    if final:
        body, out_specs, out_shape = _ffn_down_final_kernel, row_block, jax.ShapeDtypeStruct((n_rows, d), F32)
    else:
        body, out_specs = _ffn_down_kernel, [row_block, row_block]
        out_shape = [jax.ShapeDtypeStruct((n_rows, d), F32), jax.ShapeDtypeStruct((n_rows, d), BF16)]
    return pl.pallas_call(
        body,
        grid=(n_rows // tm,),
        in_specs=in_specs,
        out_specs=out_specs,
        out_shape=out_shape,
        compiler_params=_params("parallel"),
        name="ffn_down_final" if final else "ffn_down",
    )(a, w, x, g)


def _rope_pair(g128, c2):
    prod = g128 * c2
    both = prod + pltpu.roll(prod, LANES // 2, axis=1)
    lane = lax.broadcasted_iota(jnp.int32, both.shape, 1)
    return jnp.where(lane < MLA_ROPE, both, 0.0)


def _mla_down_kernel(h_ref, w_ref, gq_ref, gkv_ref, c2_ref, cq_ref, ckv_ref, kr_ref, *, q_rank, kv_rank):
    c = _dot(h_ref[...], w_ref[...])
    cq_ref[...] = _rmsnorm_f32(c[:, :q_rank], gq_ref[...]).astype(BF16)
    ckv_ref[...] = _rmsnorm_f32(c[:, q_rank:q_rank + kv_rank], gkv_ref[...]).astype(BF16)
    kr_ref[...] = _rope_pair(c[:, q_rank + kv_rank:], c2_ref[...]).astype(BF16)


def mla_down(h, w, gq, gkv, c2, *, tm, seq_len):
    n, d = h.shape
    q_rank, kv_rank = gq.shape[1], gkv.shape[1]
    width = w.shape[1]
    assert width == q_rank + kv_rank + LANES
    pos_blocks = seq_len // tm
    return pl.pallas_call(
        functools.partial(_mla_down_kernel, q_rank=q_rank, kv_rank=kv_rank),
        grid=(n // tm,),
        in_specs=[
            pl.BlockSpec((tm, d), lambda i: (i, 0)),
            _resident((d, width), lambda i: (0, 0)),
            pl.BlockSpec((1, q_rank), lambda i: (0, 0)),
            pl.BlockSpec((1, kv_rank), lambda i: (0, 0)),
            pl.BlockSpec((tm, LANES), lambda i: (i % pos_blocks, 0)),
        ],
        out_specs=[
            pl.BlockSpec((tm, q_rank), lambda i: (i, 0)),
            pl.BlockSpec((tm, kv_rank), lambda i: (i, 0)),
            pl.BlockSpec((tm, LANES), lambda i: (i, 0)),
        ],
        out_shape=[
            jax.ShapeDtypeStruct((n, q_rank), BF16),
            jax.ShapeDtypeStruct((n, kv_rank), BF16),
            jax.ShapeDtypeStruct((n, LANES), BF16),
        ],
        compiler_params=_params("parallel"),
        name="mla_down",
    )(h, w, gq, gkv, c2)


def _mla_q_kernel(cq_ref, w_ref, c2_ref, q_ref, *, scale):
    acc = _dot(cq_ref[...], w_ref[...])
    c2 = c2_ref[...]
    for hh in range(q_ref.shape[0]):
        base = hh * MLA_QK
        q_ref[hh, :, :MLA_NOPE] = (acc[:, base:base + MLA_NOPE] * scale).astype(BF16)
        rope = _rope_pair(acc[:, base + MLA_NOPE:base + MLA_QK], c2)
        q_ref[hh, :, MLA_NOPE:] = (rope * scale).astype(BF16)


def mla_q_proj(cq, w, c2, *, tm, heads_per_step, seq_len, scale):
    n, q_rank = cq.shape
    heads = w.shape[1] // MLA_QK
    tn = heads_per_step * MLA_QK
    pos_blocks = seq_len // tm
    return pl.pallas_call(
        functools.partial(_mla_q_kernel, scale=scale),
        grid=(n // tm, heads // heads_per_step),
        in_specs=[
            pl.BlockSpec((tm, q_rank), lambda i, j: (i, 0)),
            pl.BlockSpec((q_rank, tn), lambda i, j: (0, j)),
            pl.BlockSpec((tm, LANES), lambda i, j: (i % pos_blocks, 0)),
        ],
        out_specs=pl.BlockSpec((heads_per_step, tm, MLA_QK), lambda i, j: (j, i, 0)),
        out_shape=jax.ShapeDtypeStruct((heads, n, MLA_QK), BF16),
        compiler_params=_params("parallel", "parallel"),
        name="mla_q_proj",
    )(cq, w, c2)


def _mla_kv_kernel(ckv_ref, kr_ref, wk_ref, wvt_ref, k_ref, vt_ref):
    ckv = ckv_ref[...]
    kn = _dot(ckv, wk_ref[...])
    vt = _dot_nt(wvt_ref[...], ckv)
    kr = kr_ref[...]
    tm = ckv.shape[0]
    row = lax.broadcasted_iota(jnp.int32, (BF16_SUBLANES, tm), 0)
    ones_rows = jnp.where(row == 0, 1.0, 0.0).astype(BF16)
    for hh in range(k_ref.shape[0]):
        k_ref[hh, :, :MLA_NOPE] = kn[:, hh * MLA_NOPE:(hh + 1) * MLA_NOPE].astype(BF16)
        k_ref[hh, :, MLA_NOPE:] = kr
        vt_ref[hh, :MLA_V, :] = vt[hh * MLA_V:(hh + 1) * MLA_V, :].astype(BF16)
        vt_ref[hh, MLA_V:, :] = ones_rows


def mla_kv_proj(ckv, kr, wk, wvt, *, tm, heads_per_step):
    n, kv_rank = ckv.shape
    heads = wk.shape[1] // MLA_NOPE
    hps = heads_per_step
    return pl.pallas_call(
        _mla_kv_kernel,
        grid=(n // tm, heads // hps),
        in_specs=[
            pl.BlockSpec((tm, kv_rank), lambda i, j: (i, 0)),
            pl.BlockSpec((tm, LANES), lambda i, j: (i, 0)),
            pl.BlockSpec((kv_rank, hps * MLA_NOPE), lambda i, j: (0, j)),
            pl.BlockSpec((hps * MLA_V, kv_rank), lambda i, j: (j, 0)),
        ],
        out_specs=[
            pl.BlockSpec((hps, tm, MLA_QK), lambda i, j: (j, i, 0)),
            pl.BlockSpec((hps, MLA_VT_ROWS, tm), lambda i, j: (j, 0, i)),
        ],
        out_shape=[
            jax.ShapeDtypeStruct((heads, n, MLA_QK), BF16),
            jax.ShapeDtypeStruct((heads, MLA_VT_ROWS, n), BF16),
        ],
        compiler_params=_params("parallel", "parallel"),
        name="mla_kv_proj",
    )(ckv, kr, wk, wvt)


def _mla_attn_kernel(q_ref, k_ref, vt_ref, o_ref, m_ref, acc_ref, s0_ref, s1_ref, mt0_ref, mt1_ref, *, tk, unroll):
    n_kt = k_ref.shape[1] // tk
    assert unroll % 2 == 0
    s_refs, mt_refs = (s0_ref, s1_ref), (mt0_ref, mt1_ref)
    m_ref[...] = jnp.full(m_ref.shape, MASK_VALUE, F32)
    acc_ref[...] = jnp.zeros(acc_ref.shape, F32)

    def scores(kt, slot):
        start = pl.multiple_of(kt * tk, tk)
        s = _dot_nt(k_ref[0, pl.ds(start, tk), :], q_ref[0])
        s_refs[slot][...] = s
        mt_refs[slot][...] = jnp.max(s, axis=0, keepdims=True)

    def softmax_pv(kt, slot):
        start = pl.multiple_of(kt * tk, tk)
        vt = vt_ref[0, :, pl.ds(start, tk)]
        m_prev = m_ref[...]
        m_new = jnp.maximum(m_prev, mt_refs[slot][...])
        p = jnp.exp2(s_refs[slot][...] - m_new).astype(BF16)
        alpha = jnp.exp2(m_prev - m_new)
        acc_ref[...] = acc_ref[...] * alpha + _dot(vt, p)
        m_ref[...] = m_new

    scores(0, 0)
    n_loop = (n_kt - 1) // unroll

    def body(j, carry):
        for u in range(unroll):
            kt = unroll * j + u
            scores(kt + 1, (u + 1) % 2)
            softmax_pv(kt, u % 2)
        return carry

    lax.fori_loop(0, n_loop, body, 0)
    for kt in range(n_loop * unroll, n_kt):
        if kt + 1 < n_kt:
            scores(kt + 1, (kt + 1) % 2)
        softmax_pv(kt, kt % 2)
    acc = acc_ref[...]
    o_t = acc[:MLA_V, :] / acc[MLA_V:MLA_V + 1, :]
    o_ref[0] = o_t.T.astype(BF16)


def mla_attention(q_hm, k_hm, vt_hm, *, n_seq, seq_len, tq, tk, unroll):
    heads, n, _ = q_hm.shape
    nq = seq_len // tq
    return pl.pallas_call(
        functools.partial(_mla_attn_kernel, tk=tk, unroll=unroll),
        grid=(n_seq, heads, nq),
        in_specs=[
            pl.BlockSpec((1, tq, MLA_QK), lambda s, h, i: (h, s * nq + i, 0)),
            pl.BlockSpec((1, seq_len, MLA_QK), lambda s, h, i: (h, s, 0)),
            pl.BlockSpec((1, MLA_VT_ROWS, seq_len), lambda s, h, i: (h, 0, s)),
        ],
        out_specs=pl.BlockSpec((1, tq, MLA_V), lambda s, h, i: (h, s * nq + i, 0)),
        out_shape=jax.ShapeDtypeStruct((heads, n, MLA_V), BF16),
        scratch_shapes=[pltpu.VMEM((1, tq), F32), pltpu.VMEM((MLA_VT_ROWS, tq), F32),
                        pltpu.VMEM((tk, tq), F32), pltpu.VMEM((tk, tq), F32),
                        pltpu.VMEM((1, tq), F32), pltpu.VMEM((1, tq), F32)],
        compiler_params=_params("parallel", "parallel", "arbitrary"),
        name="mla_attention",
    )(q_hm, k_hm, vt_hm)


def rope_table(seq_len):
    inv_freq = ROPE_THETA ** (-jnp.arange(0, MLA_ROPE, 2, dtype=F32) / MLA_ROPE)
    ang = jnp.arange(seq_len, dtype=F32)[:, None] * inv_freq[None, :]
    cos, sin = jnp.cos(ang), jnp.sin(ang)
    return jnp.concatenate([cos, cos, -sin, sin], axis=-1)


def _swap_halves(w):
    half = w.shape[-1] // 2
    return jnp.concatenate([w[..., half:], w[..., :half]], axis=-1)


def trunk(x, seq_len, out_rows, p, tiles):
    n, d = x.shape
    n_seq = n // seq_len
    rows = seq_len // GRID_W
    row2 = lambda v: v.reshape(1, -1)

    w_qkv = p["na_w_qkv"][0].astype(BF16)
    na_dh = d // NA_HEADS
    col_scale = jnp.concatenate([jnp.full((d,), na_dh ** -0.5, F32), jnp.ones((2 * d,), F32)])
    qkv = qkv_proj(x, row2(p["g_mix"][0]), w_qkv, row2(p["na_b_qkv"][0]), row2(col_scale),
                   tm=tiles["qkv_tm"], tn=tiles["qkv_tn"])
    bias = na_bias_table(p["na_rpb"][0], rows)
    o = na_attention(qkv, bias, n_seq=n_seq, seq_len=seq_len)
    x, h = out_proj(o, p["na_w_o"][0].astype(BF16), x, row2(p["g_ffn"][0]), tm=tiles["tm"])
    a = _ffn_up(h, p, 0, tiles)
    x, h = ffn_down(a, p["ffn_w_down"][0].astype(BF16), x, row2(p["g_mix"][1]), tm=tiles["down_tm"])

    w_kr = p["mla_w_kr"][0]
    w_down = jnp.concatenate([p["mla_w_dq"][0], p["mla_w_dkv"][0], w_kr, _swap_halves(w_kr)], axis=1).astype(BF16)
    c2 = rope_table(seq_len)
    cq, ckv, kr = mla_down(h, w_down, row2(p["mla_g_q"][0]), row2(p["mla_g_kv"][0]), c2,
                           tm=tiles["tm"], seq_len=seq_len)
    q_rank = cq.shape[1]
    w_uq = p["mla_w_uq"][0].reshape(q_rank, MLA_HEADS, MLA_NOPE + MLA_ROPE)
    w_uq = jnp.concatenate([w_uq, _swap_halves(w_uq[..., MLA_NOPE:])], axis=-1)
    w_uq = w_uq.reshape(q_rank, MLA_HEADS * MLA_QK).astype(BF16)
    q_hm = mla_q_proj(cq, w_uq, c2, tm=tiles["tm"], heads_per_step=tiles["q_hps"], seq_len=seq_len,
                      scale=LOG2_E * (MLA_NOPE + MLA_ROPE) ** -0.5)
    kv_rank = ckv.shape[1]
    w_ukv = p["mla_w_ukv"][0].reshape(kv_rank, MLA_HEADS, MLA_NOPE + MLA_V)
    w_uk = w_ukv[..., :MLA_NOPE].reshape(kv_rank, MLA_HEADS * MLA_NOPE).astype(BF16)
    w_uvt = w_ukv[..., MLA_NOPE:].reshape(kv_rank, MLA_HEADS * MLA_V).T.astype(BF16)
    k_hm, vt_hm = mla_kv_proj(ckv, kr, w_uk, w_uvt, tm=tiles["tm"], heads_per_step=tiles["kv_hps"])
    o = mla_attention(q_hm, k_hm, vt_hm, n_seq=n_seq, seq_len=seq_len, tq=tiles["mla_tq"], tk=tiles["mla_tk"],
                      unroll=tiles["mla_unroll"])
    x, h = out_proj(o, p["mla_w_o"][0].astype(BF16), x, row2(p["g_ffn"][1]), tm=tiles["tm"])
    a = _ffn_up(h, p, 1, tiles)
    w_down = p["ffn_w_down"][1].astype(BF16)
    return [ffn_down(a, w_down, x, row2(p["g_final"]), tm=tiles["down_tm"], final=True, row_start=r0, n_rows=nr)
            for r0, nr in out_rows]


def _ffn_up(h, p, layer, tiles):
    return ffn_up(h, p["ffn_w_gate"][layer].astype(BF16), p["ffn_w_up"][layer].astype(BF16),
                  tm=tiles["ffn_tm"], tn=tiles["ffn_tn"])


TILES = dict(tm=512, qkv_tm=1024, qkv_tn=1024, ffn_tm=1024, ffn_tn=512, down_tm=256, q_hps=16, kv_hps=16,
             mla_tq=1024, mla_tk=1024, mla_unroll=4)


def kernel(x_prompt, x_sample, g_mix, g_ffn, g_final, na_w_qkv, na_b_qkv, na_rpb, na_w_o, mla_w_dq, mla_g_q,
           mla_w_uq, mla_w_dkv, mla_g_kv, mla_w_kr, mla_w_ukv, mla_w_o, ffn_w_gate, ffn_w_up, ffn_w_down):
    p = dict(g_mix=g_mix, g_ffn=g_ffn, g_final=g_final, na_w_qkv=na_w_qkv, na_b_qkv=na_b_qkv, na_rpb=na_rpb,
             na_w_o=na_w_o, mla_w_dq=mla_w_dq, mla_g_q=mla_g_q, mla_w_uq=mla_w_uq, mla_w_dkv=mla_w_dkv,
             mla_g_kv=mla_g_kv, mla_w_kr=mla_w_kr, mla_w_ukv=mla_w_ukv, mla_w_o=mla_w_o,
             ffn_w_gate=ffn_w_gate, ffn_w_up=ffn_w_up, ffn_w_down=ffn_w_down)
    bp, seq_len, d = x_prompt.shape
    bs = x_sample.shape[0]
    assert x_sample.shape[1:] == (seq_len, d)
    x = jnp.concatenate([x_prompt.reshape(bp * seq_len, d), x_sample.reshape(bs * seq_len, d)], axis=0)
    out_rows = [(0, bp * seq_len), (bp * seq_len, bs * seq_len)]
    y_prompt, y_sample = trunk(x, seq_len, out_rows, p, TILES)
    return (y_prompt.reshape(bp, seq_len, d), y_sample.reshape(bs, seq_len, d))
```

```python
import functools

import jax
import jax.numpy as jnp
from jax import lax
from jax.experimental import pallas as pl
from jax.experimental.pallas import tpu as pltpu

GRID_W = 64
NA_HEADS = 16
NA_KR = 8
NA_KC = 16
MLA_HEADS = 16
MLA_NOPE = 128
MLA_ROPE = 64
MLA_V = 128
ROPE_THETA = 10000.0
EPS = 1e-6
LOG2_E = 1.4426950408889634

LANES = 128
BF16_SUBLANES = 16
VMEM_LIMIT_BYTES = 56 * 1024 * 1024

NA_QROWS = 4
NA_KROWS = 12
NA_KBLK_ROWS = 4
NA_UNITS = 8
NA_WINDOW_BLOCKS = NA_UNITS + NA_KROWS // NA_KBLK_ROWS - 1
MASK_VALUE = -1e30

MLA_QK = 256
MLA_VT_ROWS = MLA_V + BF16_SUBLANES
MLA_QCHUNK = 256

F32 = jnp.float32
BF16 = jnp.bfloat16


def _params(*sem):
    return pltpu.CompilerParams(dimension_semantics=sem, vmem_limit_bytes=VMEM_LIMIT_BYTES)


def _rmsnorm_f32(x, g):
    return x * lax.rsqrt(jnp.mean(x * x, axis=-1, keepdims=True) + EPS) * g


def _dot(a, b):
    return jnp.dot(a, b, preferred_element_type=F32)


def _dot_nt(a, b):
    return lax.dot_general(a, b, (((1,), (1,)), ((), ())), preferred_element_type=F32)


def _qkv_kernel(x_ref, g_ref, w_ref, b_ref, cs_ref, o_ref, h_ref):
    @pl.when(pl.program_id(1) == 0)
    def _():
        h_ref[...] = _rmsnorm_f32(x_ref[...], g_ref[...]).astype(BF16)

    acc = (_dot(h_ref[...], w_ref[...]) + b_ref[...]) * cs_ref[...]
    for hh in range(o_ref.shape[0]):
        o_ref[hh] = acc[:, hh * LANES:(hh + 1) * LANES].astype(BF16)


def qkv_proj(x, g, w, b, cs, *, tm, tn):
    n, d = x.shape
    m = w.shape[1]
    return pl.pallas_call(
        _qkv_kernel,
        grid=(n // tm, m // tn),
        in_specs=[
            pl.BlockSpec((tm, d), lambda i, j: (i, 0)),
            pl.BlockSpec((1, d), lambda i, j: (0, 0)),
            pl.BlockSpec((d, tn), lambda i, j: (0, j)),
            pl.BlockSpec((1, tn), lambda i, j: (0, j)),
            pl.BlockSpec((1, tn), lambda i, j: (0, j)),
        ],
        out_specs=pl.BlockSpec((tn // LANES, tm, LANES), lambda i, j: (j, i, 0)),
        out_shape=jax.ShapeDtypeStruct((m // LANES, n, LANES), BF16),
        scratch_shapes=[pltpu.VMEM((tm, d), BF16)],
        compiler_params=_params("parallel", "arbitrary"),
        name="qkv_proj",
    )(x, g, w, b, cs)


def _na_kernel(q_ref, k_ref, v_ref, bias_ref, o_ref, *, n_units):
    step = pl.program_id(2)
    tq = NA_QROWS * GRID_W
    tk = NA_KROWS * GRID_W
    window_base = _na_window_base(step, n_units)
    for u in range(NA_UNITS):
        unit = NA_UNITS * step + u
        start = pl.multiple_of((_na_key_base(unit, n_units) - window_base) * tq, tq)
        k = k_ref[0, pl.ds(start, tk), :]
        v = v_ref[0, pl.ds(start, tk), :]
        cls = jnp.where(unit == 0, 0, jnp.where(unit == n_units - 1, 2, 1))
        rows = pl.ds(u * tq, tq)
        s = _dot_nt(q_ref[0, rows, :], k) + bias_ref[cls, 0]
        m = jnp.max(s, axis=-1, keepdims=True)
        p = jnp.exp(s - m)
        l = jnp.sum(p, axis=-1, keepdims=True)
        o = _dot(p.astype(BF16), v)
        o_ref[0, rows, :] = (o / l).astype(BF16)


def _na_key_base(unit, n_units):
    return jnp.clip(unit - 1, 0, n_units - NA_KROWS // NA_KBLK_ROWS)


def _na_window_base(step, n_units):
    return jnp.clip(NA_UNITS * step - 1, 0, n_units - NA_WINDOW_BLOCKS)


def na_bias_table(rpb, rows):
    heads = rpb.shape[0]
    nb = rows // NA_QROWS
    cols = jnp.arange(GRID_W)
    c_start = jnp.clip(cols - NA_KC // 2, 0, GRID_W - NA_KC)
    cvalid = (cols[None, :] >= c_start[:, None]) & (cols[None, :] < c_start[:, None] + NA_KC)
    coff = jnp.clip(cols[None, :] - cols[:, None] + (NA_KC - 1), 0, 2 * NA_KC - 2)
    a = rpb[:, :, coff]
    tables = []
    for b in (0, 1, nb - 1):
        r = b * NA_QROWS + jnp.arange(NA_QROWS)
        r_start = jnp.clip(r - NA_KR // 2, 0, rows - NA_KR)
        kb = NA_KBLK_ROWS * min(max(b - 1, 0), nb - NA_KROWS // NA_KBLK_ROWS)
        krow = kb + jnp.arange(NA_KROWS)
        rvalid = (krow[None, :] >= r_start[:, None]) & (krow[None, :] < r_start[:, None] + NA_KR)
        roff = jnp.clip(krow[None, :] - r[:, None] + (NA_KR - 1), 0, 2 * NA_KR - 2)
        t = a[:, roff]
        valid = rvalid[:, :, None, None] & cvalid[None, None]
        t = jnp.where(valid[None], t, MASK_VALUE)
        t = t.transpose(0, 1, 3, 2, 4).reshape(heads, NA_QROWS * GRID_W, NA_KROWS * GRID_W)
        tables.append(t)
    return jnp.stack(tables).astype(F32)


def na_attention(qkv_hm, bias, *, n_seq, seq_len):
    heads = NA_HEADS
    rows = seq_len // GRID_W
    assert NA_QROWS == NA_KBLK_ROWS and rows % (NA_QROWS * NA_UNITS) == 0
    n_units = rows // NA_QROWS
    assert n_units >= NA_WINDOW_BLOCKS
    nb = n_units // NA_UNITS
    tq = NA_QROWS * GRID_W
    n = n_seq * seq_len

    def q_map(h, s, b):
        return (h, s * nb + b, 0)

    def kv_map(which):
        def f(h, s, b):
            return (which * heads + h, (s * n_units + _na_window_base(b, n_units)) * tq, 0)
        return f

    kv_block = (pl.Element(1), pl.Element(NA_WINDOW_BLOCKS * tq), pl.Element(LANES))
    return pl.pallas_call(
        functools.partial(_na_kernel, n_units=n_units),
        grid=(heads, n_seq, nb),
        in_specs=[
            pl.BlockSpec((1, NA_UNITS * tq, LANES), q_map),
            pl.BlockSpec(kv_block, kv_map(1)),
            pl.BlockSpec(kv_block, kv_map(2)),
            pl.BlockSpec((3, 1, tq, NA_KROWS * GRID_W), lambda h, s, b: (0, h, 0, 0)),
        ],
        out_specs=pl.BlockSpec((1, NA_UNITS * tq, LANES), q_map),
        out_shape=jax.ShapeDtypeStruct((heads, n, LANES), BF16),
        compiler_params=_params("parallel", "parallel", "arbitrary"),
        name="na_attention",
    )(qkv_hm, qkv_hm, qkv_hm, bias)


def _resident(block_shape, index_map):
    return pl.BlockSpec(block_shape, index_map, pipeline_mode=pl.Buffered(1))


def _oproj_kernel(o_ref, w_ref, x_ref, g_ref, y_ref, h_ref):
    o = jnp.concatenate([o_ref[h] for h in range(o_ref.shape[0])], axis=-1)
    y = x_ref[...] + _dot(o, w_ref[...])
    y_ref[...] = y
    h_ref[...] = _rmsnorm_f32(y, g_ref[...]).astype(BF16)


def out_proj(o_hm, w, x, g, *, tm):
    heads, n, dh = o_hm.shape
    d = w.shape[1]
    return pl.pallas_call(
        _oproj_kernel,
        grid=(n // tm,),
        in_specs=[
            pl.BlockSpec((heads, tm, dh), lambda i: (0, i, 0)),
            _resident((heads * dh, d), lambda i: (0, 0)),
            pl.BlockSpec((tm, d), lambda i: (i, 0)),
            pl.BlockSpec((1, d), lambda i: (0, 0)),
        ],
        out_specs=[pl.BlockSpec((tm, d), lambda i: (i, 0)), pl.BlockSpec((tm, d), lambda i: (i, 0))],
        out_shape=[jax.ShapeDtypeStruct((n, d), F32), jax.ShapeDtypeStruct((n, d), BF16)],
        compiler_params=_params("parallel"),
        name="out_proj",
    )(o_hm, w, x, g)


def _ffn_up_kernel(h_ref, wg_ref, wu_ref, a_ref):
    h = h_ref[...]
    gate = _dot(h, wg_ref[...])
    up = _dot(h, wu_ref[...])
    a_ref[...] = (gate * jax.nn.sigmoid(gate) * up).astype(BF16)


def ffn_up(h, wg, wu, *, tm, tn):
    n, d = h.shape
    f = wg.shape[1]
    return pl.pallas_call(
        _ffn_up_kernel,
        grid=(n // tm, f // tn),
        in_specs=[
            pl.BlockSpec((tm, d), lambda i, j: (i, 0)),
            pl.BlockSpec((d, tn), lambda i, j: (0, j)),
            pl.BlockSpec((d, tn), lambda i, j: (0, j)),
        ],
        out_specs=pl.BlockSpec((tm, tn), lambda i, j: (i, j)),
        out_shape=jax.ShapeDtypeStruct((n, f), BF16),
        compiler_params=_params("parallel", "parallel"),
        name="ffn_up",
    )(h, wg, wu)


def _ffn_down_kernel(a_ref, w_ref, x_ref, g_ref, y_ref, h_ref):
    y = x_ref[...] + _dot(a_ref[...], w_ref[...])
    y_ref[...] = y
    h_ref[...] = _rmsnorm_f32(y, g_ref[...]).astype(BF16)


def _ffn_down_final_kernel(a_ref, w_ref, x_ref, g_ref, y_ref):
    y_ref[...] = _rmsnorm_f32(x_ref[...] + _dot(a_ref[...], w_ref[...]), g_ref[...])


def ffn_down(a, w, x, g, *, tm, final=False, row_start=0, n_rows=None):
    f = a.shape[1]
    d = w.shape[1]
    n_rows = a.shape[0] if n_rows is None else n_rows
    off = row_start // tm
    assert row_start % tm == 0 and n_rows % tm == 0
    in_specs = [
        pl.BlockSpec((tm, f), lambda i: (i + off, 0)),
        _resident((f, d), lambda i: (0, 0)),
        pl.BlockSpec((tm, d), lambda i: (i + off, 0)),
        pl.BlockSpec((1, d), lambda i: (0, 0)),
    ]
    row_block = pl.BlockSpec((tm, d), lambda i: (i, 0))
    if final:
        body, out_specs, out_shape = _ffn_down_final_kernel, row_block, jax.ShapeDtypeStruct((n_rows, d), F32)
    else:
        body, out_specs = _ffn_down_kernel, [row_block, row_block]
        out_shape = [jax.ShapeDtypeStruct((n_rows, d), F32), jax.ShapeDtypeStruct((n_rows, d), BF16)]
    return pl.pallas_call(
        body,
        grid=(n_rows // tm,),
        in_specs=in_specs,
        out_specs=out_specs,
        out_shape=out_shape,
        compiler_params=_params("parallel"),
        name="ffn_down_final" if final else "ffn_down",
    )(a, w, x, g)


def _rope_pair(g128, c2):
    prod = g128 * c2
    both = prod + pltpu.roll(prod, LANES // 2, axis=1)
    lane = lax.broadcasted_iota(jnp.int32, both.shape, 1)
    return jnp.where(lane < MLA_ROPE, both, 0.0)


def _mla_down_kernel(h_ref, w_ref, gq_ref, gkv_ref, c2_ref, cq_ref, ckv_ref, kr_ref, *, q_rank, kv_rank):
    c = _dot(h_ref[...], w_ref[...])
    cq_ref[...] = _rmsnorm_f32(c[:, :q_rank], gq_ref[...]).astype(BF16)
    ckv_ref[...] = _rmsnorm_f32(c[:, q_rank:q_rank + kv_rank], gkv_ref[...]).astype(BF16)
    kr_ref[...] = _rope_pair(c[:, q_rank + kv_rank:], c2_ref[...]).astype(BF16)


def mla_down(h, w, gq, gkv, c2, *, tm, seq_len):
    n, d = h.shape
    q_rank, kv_rank = gq.shape[1], gkv.shape[1]
    width = w.shape[1]
    assert width == q_rank + kv_rank + LANES
    pos_blocks = seq_len // tm
    return pl.pallas_call(
        functools.partial(_mla_down_kernel, q_rank=q_rank, kv_rank=kv_rank),
        grid=(n // tm,),
        in_specs=[
            pl.BlockSpec((tm, d), lambda i: (i, 0)),
            _resident((d, width), lambda i: (0, 0)),
            pl.BlockSpec((1, q_rank), lambda i: (0, 0)),
            pl.BlockSpec((1, kv_rank), lambda i: (0, 0)),
            pl.BlockSpec((tm, LANES), lambda i: (i % pos_blocks, 0)),
        ],
        out_specs=[
            pl.BlockSpec((tm, q_rank), lambda i: (i, 0)),
            pl.BlockSpec((tm, kv_rank), lambda i: (i, 0)),
            pl.BlockSpec((tm, LANES), lambda i: (i, 0)),
        ],
        out_shape=[
            jax.ShapeDtypeStruct((n, q_rank), BF16),
            jax.ShapeDtypeStruct((n, kv_rank), BF16),
            jax.ShapeDtypeStruct((n, LANES), BF16),
        ],
        compiler_params=_params("parallel"),
        name="mla_down",
    )(h, w, gq, gkv, c2)


def _mla_q_kernel(cq_ref, w_ref, c2_ref, q_ref, *, scale):
    acc = _dot(cq_ref[...], w_ref[...])
    c2 = c2_ref[...]
    for hh in range(q_ref.shape[0]):
        base = hh * MLA_QK
        q_ref[hh, :, :MLA_NOPE] = (acc[:, base:base + MLA_NOPE] * scale).astype(BF16)
        rope = _rope_pair(acc[:, base + MLA_NOPE:base + MLA_QK], c2)
        q_ref[hh, :, MLA_NOPE:] = (rope * scale).astype(BF16)


def mla_q_proj(cq, w, c2, *, tm, heads_per_step, seq_len, scale):
    n, q_rank = cq.shape
    heads = w.shape[1] // MLA_QK
    tn = heads_per_step * MLA_QK
    pos_blocks = seq_len // tm
    return pl.pallas_call(
        functools.partial(_mla_q_kernel, scale=scale),
        grid=(n // tm, heads // heads_per_step),
        in_specs=[
            pl.BlockSpec((tm, q_rank), lambda i, j: (i, 0)),
            pl.BlockSpec((q_rank, tn), lambda i, j: (0, j)),
            pl.BlockSpec((tm, LANES), lambda i, j: (i % pos_blocks, 0)),
        ],
        out_specs=pl.BlockSpec((heads_per_step, tm, MLA_QK), lambda i, j: (j, i, 0)),
        out_shape=jax.ShapeDtypeStruct((heads, n, MLA_QK), BF16),
        compiler_params=_params("parallel", "parallel"),
        name="mla_q_proj",
    )(cq, w, c2)


def _mla_kv_kernel(ckv_ref, kr_ref, wk_ref, wvt_ref, k_ref, vt_ref):
    ckv = ckv_ref[...]
    kn = _dot(ckv, wk_ref[...])
    vt = _dot_nt(wvt_ref[...], ckv)
    kr = kr_ref[...]
    tm = ckv.shape[0]
    row = lax.broadcasted_iota(jnp.int32, (BF16_SUBLANES, tm), 0)
    ones_rows = jnp.where(row == 0, 1.0, 0.0).astype(BF16)
    for hh in range(k_ref.shape[0]):
        k_ref[hh, :, :MLA_NOPE] = kn[:, hh * MLA_NOPE:(hh + 1) * MLA_NOPE].astype(BF16)
        k_ref[hh, :, MLA_NOPE:] = kr
        vt_ref[hh, :MLA_V, :] = vt[hh * MLA_V:(hh + 1) * MLA_V, :].astype(BF16)
        vt_ref[hh, MLA_V:, :] = ones_rows


def mla_kv_proj(ckv, kr, wk, wvt, *, tm, heads_per_step):
    n, kv_rank = ckv.shape
    heads = wk.shape[1] // MLA_NOPE
    hps = heads_per_step
    return pl.pallas_call(
        _mla_kv_kernel,
        grid=(n // tm, heads // hps),
        in_specs=[
            pl.BlockSpec((tm, kv_rank), lambda i, j: (i, 0)),
            pl.BlockSpec((tm, LANES), lambda i, j: (i, 0)),
            pl.BlockSpec((kv_rank, hps * MLA_NOPE), lambda i, j: (0, j)),
            pl.BlockSpec((hps * MLA_V, kv_rank), lambda i, j: (j, 0)),
        ],
        out_specs=[
            pl.BlockSpec((hps, tm, MLA_QK), lambda i, j: (j, i, 0)),
            pl.BlockSpec((hps, MLA_VT_ROWS, tm), lambda i, j: (j, 0, i)),
        ],
        out_shape=[
            jax.ShapeDtypeStruct((heads, n, MLA_QK), BF16),
            jax.ShapeDtypeStruct((heads, MLA_VT_ROWS, n), BF16),
        ],
        compiler_params=_params("parallel", "parallel"),
        name="mla_kv_proj",
    )(ckv, kr, wk, wvt)


def _mla_attn_kernel(q_ref, k_ref, vt_ref, o_ref, m_ref, acc_ref, s0_ref, s1_ref, mt0_ref, mt1_ref, *, tk, unroll):
    n_kt = k_ref.shape[1] // tk
    assert unroll % 2 == 0
    s_refs, mt_refs = (s0_ref, s1_ref), (mt0_ref, mt1_ref)
    m_ref[...] = jnp.full(m_ref.shape, MASK_VALUE, F32)
    acc_ref[...] = jnp.zeros(acc_ref.shape, F32)

    tq = q_ref.shape[1]
    chunks = [pl.ds(c * MLA_QCHUNK, MLA_QCHUNK) for c in range(tq // MLA_QCHUNK)]

    def scores(kt, slot, cols):
        start = pl.multiple_of(kt * tk, tk)
        s = _dot_nt(k_ref[0, pl.ds(start, tk), :], q_ref[0, cols, :])
        s_refs[slot][:, cols] = s
        mt_refs[slot][:, cols] = jnp.max(s, axis=0, keepdims=True)

    def probs(slot, cols):
        m_prev = m_ref[:, cols]
        m_new = jnp.maximum(m_prev, mt_refs[slot][:, cols])
        m_ref[:, cols] = m_new
        return jnp.exp2(s_refs[slot][:, cols] - m_new).astype(BF16), jnp.exp2(m_prev - m_new)

    def accumulate(kt, cols, p, alpha):
        start = pl.multiple_of(kt * tk, tk)
        acc_ref[:, cols] = acc_ref[:, cols] * alpha + _dot(vt_ref[0, :, pl.ds(start, tk)], p)

    def step(kt, slot, with_next):
        for cols in chunks:
            p, alpha = probs(slot, cols)
            if with_next:
                scores(kt + 1, 1 - slot, cols)
            accumulate(kt, cols, p, alpha)

    for cols in chunks:
        scores(0, 0, cols)
    n_loop = (n_kt - 1) // unroll

    def body(j, carry):
        for u in range(unroll):
            step(unroll * j + u, u % 2, True)
        return carry

    lax.fori_loop(0, n_loop, body, 0)
    for kt in range(n_loop * unroll, n_kt):
        step(kt, kt % 2, kt + 1 < n_kt)
    acc = acc_ref[...]
    o_t = acc[:MLA_V, :] / acc[MLA_V:MLA_V + 1, :]
    o_ref[0] = o_t.T.astype(BF16)


def mla_attention(q_hm, k_hm, vt_hm, *, n_seq, seq_len, tq, tk, unroll):
    heads, n, _ = q_hm.shape
    nq = seq_len // tq
    return pl.pallas_call(
        functools.partial(_mla_attn_kernel, tk=tk, unroll=unroll),
        grid=(n_seq, heads, nq),
        in_specs=[
            pl.BlockSpec((1, tq, MLA_QK), lambda s, h, i: (h, s * nq + i, 0)),
            pl.BlockSpec((1, seq_len, MLA_QK), lambda s, h, i: (h, s, 0)),
            pl.BlockSpec((1, MLA_VT_ROWS, seq_len), lambda s, h, i: (h, 0, s)),
        ],
        out_specs=pl.BlockSpec((1, tq, MLA_V), lambda s, h, i: (h, s * nq + i, 0)),
        out_shape=jax.ShapeDtypeStruct((heads, n, MLA_V), BF16),
        scratch_shapes=[pltpu.VMEM((1, tq), F32), pltpu.VMEM((MLA_VT_ROWS, tq), F32),
                        pltpu.VMEM((tk, tq), F32), pltpu.VMEM((tk, tq), F32),
                        pltpu.VMEM((1, tq), F32), pltpu.VMEM((1, tq), F32)],
        compiler_params=_params("parallel", "parallel", "arbitrary"),
        name="mla_attention",
    )(q_hm, k_hm, vt_hm)


def rope_table(seq_len):
    inv_freq = ROPE_THETA ** (-jnp.arange(0, MLA_ROPE, 2, dtype=F32) / MLA_ROPE)
    ang = jnp.arange(seq_len, dtype=F32)[:, None] * inv_freq[None, :]
    cos, sin = jnp.cos(ang), jnp.sin(ang)
    return jnp.concatenate([cos, cos, -sin, sin], axis=-1)


def _swap_halves(w):
    half = w.shape[-1] // 2
    return jnp.concatenate([w[..., half:], w[..., :half]], axis=-1)


def trunk(x, seq_len, out_rows, p, tiles):
    n, d = x.shape
    n_seq = n // seq_len
    rows = seq_len // GRID_W
    row2 = lambda v: v.reshape(1, -1)

    w_qkv = p["na_w_qkv"][0].astype(BF16)
    na_dh = d // NA_HEADS
    col_scale = jnp.concatenate([jnp.full((d,), na_dh ** -0.5, F32), jnp.ones((2 * d,), F32)])
    qkv = qkv_proj(x, row2(p["g_mix"][0]), w_qkv, row2(p["na_b_qkv"][0]), row2(col_scale),
                   tm=tiles["qkv_tm"], tn=tiles["qkv_tn"])
    bias = na_bias_table(p["na_rpb"][0], rows)
    o = na_attention(qkv, bias, n_seq=n_seq, seq_len=seq_len)
    x, h = out_proj(o, p["na_w_o"][0].astype(BF16), x, row2(p["g_ffn"][0]), tm=tiles["tm"])
    a = _ffn_up(h, p, 0, tiles)
    x, h = ffn_down(a, p["ffn_w_down"][0].astype(BF16), x, row2(p["g_mix"][1]), tm=tiles["down_tm"])

    w_kr = p["mla_w_kr"][0]
    w_down = jnp.concatenate([p["mla_w_dq"][0], p["mla_w_dkv"][0], w_kr, _swap_halves(w_kr)], axis=1).astype(BF16)
    c2 = rope_table(seq_len)
    cq, ckv, kr = mla_down(h, w_down, row2(p["mla_g_q"][0]), row2(p["mla_g_kv"][0]), c2,
                           tm=tiles["tm"], seq_len=seq_len)
    q_rank = cq.shape[1]
    w_uq = p["mla_w_uq"][0].reshape(q_rank, MLA_HEADS, MLA_NOPE + MLA_ROPE)
    w_uq = jnp.concatenate([w_uq, _swap_halves(w_uq[..., MLA_NOPE:])], axis=-1)
    w_uq = w_uq.reshape(q_rank, MLA_HEADS * MLA_QK).astype(BF16)
    q_hm = mla_q_proj(cq, w_uq, c2, tm=tiles["tm"], heads_per_step=tiles["q_hps"], seq_len=seq_len,
                      scale=LOG2_E * (MLA_NOPE + MLA_ROPE) ** -0.5)
    kv_rank = ckv.shape[1]
    w_ukv = p["mla_w_ukv"][0].reshape(kv_rank, MLA_HEADS, MLA_NOPE + MLA_V)
    w_uk = w_ukv[..., :MLA_NOPE].reshape(kv_rank, MLA_HEADS * MLA_NOPE).astype(BF16)
    w_uvt = w_ukv[..., MLA_NOPE:].reshape(kv_rank, MLA_HEADS * MLA_V).T.astype(BF16)
    k_hm, vt_hm = mla_kv_proj(ckv, kr, w_uk, w_uvt, tm=tiles["tm"], heads_per_step=tiles["kv_hps"])
    o = mla_attention(q_hm, k_hm, vt_hm, n_seq=n_seq, seq_len=seq_len, tq=tiles["mla_tq"], tk=tiles["mla_tk"],
                      unroll=tiles["mla_unroll"])
    x, h = out_proj(o, p["mla_w_o"][0].astype(BF16), x, row2(p["g_ffn"][1]), tm=tiles["tm"])
    a = _ffn_up(h, p, 1, tiles)
    w_down = p["ffn_w_down"][1].astype(BF16)
    return [ffn_down(a, w_down, x, row2(p["g_final"]), tm=tiles["down_tm"], final=True, row_start=r0, n_rows=nr)
            for r0, nr in out_rows]


def _ffn_up(h, p, layer, tiles):
    return ffn_up(h, p["ffn_w_gate"][layer].astype(BF16), p["ffn_w_up"][layer].astype(BF16),
                  tm=tiles["ffn_tm"], tn=tiles["ffn_tn"])


TILES = dict(tm=512, qkv_tm=1024, qkv_tn=1024, ffn_tm=1024, ffn_tn=512, down_tm=256, q_hps=16, kv_hps=16,
             mla_tq=1024, mla_tk=1024, mla_unroll=4)


def kernel(x_prompt, x_sample, g_mix, g_ffn, g_final, na_w_qkv, na_b_qkv, na_rpb, na_w_o, mla_w_dq, mla_g_q,
           mla_w_uq, mla_w_dkv, mla_g_kv, mla_w_kr, mla_w_ukv, mla_w_o, ffn_w_gate, ffn_w_up, ffn_w_down):
    p = dict(g_mix=g_mix, g_ffn=g_ffn, g_final=g_final, na_w_qkv=na_w_qkv, na_b_qkv=na_b_qkv, na_rpb=na_rpb,
             na_w_o=na_w_o, mla_w_dq=mla_w_dq, mla_g_q=mla_g_q, mla_w_uq=mla_w_uq, mla_w_dkv=mla_w_dkv,
             mla_g_kv=mla_g_kv, mla_w_kr=mla_w_kr, mla_w_ukv=mla_w_ukv, mla_w_o=mla_w_o,
             ffn_w_gate=ffn_w_gate, ffn_w_up=ffn_w_up, ffn_w_down=ffn_w_down)
    bp, seq_len, d = x_prompt.shape
    bs = x_sample.shape[0]
    assert x_sample.shape[1:] == (seq_len, d)
    x = jnp.concatenate([x_prompt.reshape(bp * seq_len, d), x_sample.reshape(bs * seq_len, d)], axis=0)
    out_rows = [(0, bp * seq_len), (bp * seq_len, bs * seq_len)]
    y_prompt, y_sample = trunk(x, seq_len, out_rows, p, TILES)
    return (y_prompt.reshape(bp, seq_len, d), y_sample.reshape(bs, seq_len, d))
```

```python
import functools

import jax
import jax.numpy as jnp
from jax import lax
from jax.experimental import pallas as pl
from jax.experimental.pallas import tpu as pltpu

GRID_W = 64
NA_HEADS = 16
NA_KR = 8
NA_KC = 16
MLA_HEADS = 16
MLA_NOPE = 128
MLA_ROPE = 64
MLA_V = 128
ROPE_THETA = 10000.0
EPS = 1e-6
LOG2_E = 1.4426950408889634

LANES = 128
BF16_SUBLANES = 16
VMEM_LIMIT_BYTES = 56 * 1024 * 1024

NA_QROWS = 4
NA_KROWS = 12
NA_KBLK_ROWS = 4
NA_UNITS = 16
NA_WINDOW_BLOCKS = NA_UNITS + NA_KROWS // NA_KBLK_ROWS - 1
MASK_VALUE = -1e30

MLA_QK = 256
MLA_VT_ROWS = MLA_V + BF16_SUBLANES
MLA_QCHUNK = 256

F32 = jnp.float32
BF16 = jnp.bfloat16


def _params(*sem):
    return pltpu.CompilerParams(dimension_semantics=sem, vmem_limit_bytes=VMEM_LIMIT_BYTES)


def _rmsnorm_f32(x, g):
    return x * lax.rsqrt(jnp.mean(x * x, axis=-1, keepdims=True) + EPS) * g


def _dot(a, b):
    return jnp.dot(a, b, preferred_element_type=F32)


def _dot_nt(a, b):
    return lax.dot_general(a, b, (((1,), (1,)), ((), ())), preferred_element_type=F32)


def _qkv_kernel(x_ref, g_ref, w_ref, b_ref, cs_ref, o_ref, h_ref):
    @pl.when(pl.program_id(1) == 0)
    def _():
        h_ref[...] = _rmsnorm_f32(x_ref[...], g_ref[...]).astype(BF16)

    acc = (_dot(h_ref[...], w_ref[...]) + b_ref[...]) * cs_ref[...]
    for hh in range(o_ref.shape[0]):
        o_ref[hh] = acc[:, hh * LANES:(hh + 1) * LANES].astype(BF16)


def qkv_proj(x, g, w, b, cs, *, tm, tn):
    n, d = x.shape
    m = w.shape[1]
    return pl.pallas_call(
        _qkv_kernel,
        grid=(n // tm, m // tn),
        in_specs=[
            pl.BlockSpec((tm, d), lambda i, j: (i, 0)),
            pl.BlockSpec((1, d), lambda i, j: (0, 0)),
            pl.BlockSpec((d, tn), lambda i, j: (0, j)),
            pl.BlockSpec((1, tn), lambda i, j: (0, j)),
            pl.BlockSpec((1, tn), lambda i, j: (0, j)),
        ],
        out_specs=pl.BlockSpec((tn // LANES, tm, LANES), lambda i, j: (j, i, 0)),
        out_shape=jax.ShapeDtypeStruct((m // LANES, n, LANES), BF16),
        scratch_shapes=[pltpu.VMEM((tm, d), BF16)],
        compiler_params=_params("parallel", "arbitrary"),
        name="qkv_proj",
    )(x, g, w, b, cs)


def _na_kernel(q_ref, k_ref, v_ref, bias_ref, o_ref, *, n_units):
    step = pl.program_id(2)
    tq = NA_QROWS * GRID_W
    tk = NA_KROWS * GRID_W
    window_base = _na_window_base(step, n_units)
    for u in range(NA_UNITS):
        unit = NA_UNITS * step + u
        start = pl.multiple_of((_na_key_base(unit, n_units) - window_base) * tq, tq)
        k = k_ref[0, pl.ds(start, tk), :]
        v = v_ref[0, pl.ds(start, tk), :]
        cls = jnp.where(unit == 0, 0, jnp.where(unit == n_units - 1, 2, 1))
        rows = pl.ds(u * tq, tq)
        s = _dot_nt(q_ref[0, rows, :], k) + bias_ref[cls, 0]
        m = jnp.max(s, axis=-1, keepdims=True)
        p = jnp.exp2(s - m)
        l = jnp.sum(p, axis=-1, keepdims=True)
        o = _dot(p.astype(BF16), v)
        o_ref[0, rows, :] = (o / l).astype(BF16)


def _na_key_base(unit, n_units):
    return jnp.clip(unit - 1, 0, n_units - NA_KROWS // NA_KBLK_ROWS)


def _na_window_base(step, n_units):
    return jnp.clip(NA_UNITS * step - 1, 0, n_units - NA_WINDOW_BLOCKS)


def na_bias_table(rpb, rows):
    heads = rpb.shape[0]
    nb = rows // NA_QROWS
    cols = jnp.arange(GRID_W)
    c_start = jnp.clip(cols - NA_KC // 2, 0, GRID_W - NA_KC)
    cvalid = (cols[None, :] >= c_start[:, None]) & (cols[None, :] < c_start[:, None] + NA_KC)
    coff = jnp.clip(cols[None, :] - cols[:, None] + (NA_KC - 1), 0, 2 * NA_KC - 2)
    a = rpb[:, :, coff]
    tables = []
    for b in (0, 1, nb - 1):
        r = b * NA_QROWS + jnp.arange(NA_QROWS)
        r_start = jnp.clip(r - NA_KR // 2, 0, rows - NA_KR)
        kb = NA_KBLK_ROWS * min(max(b - 1, 0), nb - NA_KROWS // NA_KBLK_ROWS)
        krow = kb + jnp.arange(NA_KROWS)
        rvalid = (krow[None, :] >= r_start[:, None]) & (krow[None, :] < r_start[:, None] + NA_KR)
        roff = jnp.clip(krow[None, :] - r[:, None] + (NA_KR - 1), 0, 2 * NA_KR - 2)
        t = a[:, roff]
        valid = rvalid[:, :, None, None] & cvalid[None, None]
        t = jnp.where(valid[None], t * LOG2_E, MASK_VALUE)
        t = t.transpose(0, 1, 3, 2, 4).reshape(heads, NA_QROWS * GRID_W, NA_KROWS * GRID_W)
        tables.append(t)
    return jnp.stack(tables).astype(F32)


def na_attention(qkv_hm, bias, *, n_seq, seq_len):
    heads = NA_HEADS
    rows = seq_len // GRID_W
    assert NA_QROWS == NA_KBLK_ROWS and rows % (NA_QROWS * NA_UNITS) == 0
    n_units = rows // NA_QROWS
    assert n_units >= NA_WINDOW_BLOCKS
    nb = n_units // NA_UNITS
    tq = NA_QROWS * GRID_W
    n = n_seq * seq_len

    def q_map(h, s, b):
        return (h, s * nb + b, 0)

    def kv_map(which):
        def f(h, s, b):
            return (which * heads + h, (s * n_units + _na_window_base(b, n_units)) * tq, 0)
        return f

    kv_block = (pl.Element(1), pl.Element(NA_WINDOW_BLOCKS * tq), pl.Element(LANES))
    return pl.pallas_call(
        functools.partial(_na_kernel, n_units=n_units),
        grid=(heads, n_seq, nb),
        in_specs=[
            pl.BlockSpec((1, NA_UNITS * tq, LANES), q_map),
            pl.BlockSpec(kv_block, kv_map(1)),
            pl.BlockSpec(kv_block, kv_map(2)),
            pl.BlockSpec((3, 1, tq, NA_KROWS * GRID_W), lambda h, s, b: (0, h, 0, 0)),
        ],
        out_specs=pl.BlockSpec((1, NA_UNITS * tq, LANES), q_map),
        out_shape=jax.ShapeDtypeStruct((heads, n, LANES), BF16),
        compiler_params=_params("parallel", "parallel", "arbitrary"),
        name="na_attention",
    )(qkv_hm, qkv_hm, qkv_hm, bias)


def _resident(block_shape, index_map):
    return pl.BlockSpec(block_shape, index_map, pipeline_mode=pl.Buffered(1))


def _oproj_kernel(o_ref, w_ref, x_ref, g_ref, y_ref, h_ref):
    o = jnp.concatenate([o_ref[h] for h in range(o_ref.shape[0])], axis=-1)
    y = x_ref[...] + _dot(o, w_ref[...])
    y_ref[...] = y
    h_ref[...] = _rmsnorm_f32(y, g_ref[...]).astype(BF16)


def out_proj(o_hm, w, x, g, *, tm):
    heads, n, dh = o_hm.shape
    d = w.shape[1]
    return pl.pallas_call(
        _oproj_kernel,
        grid=(n // tm,),
        in_specs=[
            pl.BlockSpec((heads, tm, dh), lambda i: (0, i, 0)),
            _resident((heads * dh, d), lambda i: (0, 0)),
            pl.BlockSpec((tm, d), lambda i: (i, 0)),
            pl.BlockSpec((1, d), lambda i: (0, 0)),
        ],
        out_specs=[pl.BlockSpec((tm, d), lambda i: (i, 0)), pl.BlockSpec((tm, d), lambda i: (i, 0))],
        out_shape=[jax.ShapeDtypeStruct((n, d), F32), jax.ShapeDtypeStruct((n, d), BF16)],
        compiler_params=_params("parallel"),
        name="out_proj",
    )(o_hm, w, x, g)


def _ffn_up_kernel(h_ref, wg_ref, wu_ref, a_ref):
    h = h_ref[...]
    gate = _dot(h, wg_ref[...])
    up = _dot(h, wu_ref[...])
    a_ref[...] = (gate * jax.nn.sigmoid(gate) * up).astype(BF16)


def ffn_up(h, wg, wu, *, tm, tn):
    n, d = h.shape
    f = wg.shape[1]
    return pl.pallas_call(
        _ffn_up_kernel,
        grid=(n // tm, f // tn),
        in_specs=[
            pl.BlockSpec((tm, d), lambda i, j: (i, 0)),
            pl.BlockSpec((d, tn), lambda i, j: (0, j)),
            pl.BlockSpec((d, tn), lambda i, j: (0, j)),
        ],
        out_specs=pl.BlockSpec((tm, tn), lambda i, j: (i, j)),
        out_shape=jax.ShapeDtypeStruct((n, f), BF16),
        compiler_params=_params("parallel", "parallel"),
        name="ffn_up",
    )(h, wg, wu)


def _ffn_down_kernel(a_ref, w_ref, x_ref, g_ref, y_ref, h_ref):
    y = x_ref[...] + _dot(a_ref[...], w_ref[...])
    y_ref[...] = y
    h_ref[...] = _rmsnorm_f32(y, g_ref[...]).astype(BF16)


def _ffn_down_final_kernel(a_ref, w_ref, x_ref, g_ref, y_ref):
    y_ref[...] = _rmsnorm_f32(x_ref[...] + _dot(a_ref[...], w_ref[...]), g_ref[...])


def ffn_down(a, w, x, g, *, tm, final=False, row_start=0, n_rows=None):
    f = a.shape[1]
    d = w.shape[1]
    n_rows = a.shape[0] if n_rows is None else n_rows
    off = row_start // tm
    assert row_start % tm == 0 and n_rows % tm == 0
    in_specs = [
        pl.BlockSpec((tm, f), lambda i: (i + off, 0)),
        _resident((f, d), lambda i: (0, 0)),
        pl.BlockSpec((tm, d), lambda i: (i + off, 0)),
        pl.BlockSpec((1, d), lambda i: (0, 0)),
    ]
    row_block = pl.BlockSpec((tm, d), lambda i: (i, 0))
    if final:
        body, out_specs, out_shape = _ffn_down_final_kernel, row_block, jax.ShapeDtypeStruct((n_rows, d), F32)
    else:
        body, out_specs = _ffn_down_kernel, [row_block, row_block]
        out_shape = [jax.ShapeDtypeStruct((n_rows, d), F32), jax.ShapeDtypeStruct((n_rows, d), BF16)]
    return pl.pallas_call(
        body,
        grid=(n_rows // tm,),
        in_specs=in_specs,
        out_specs=out_specs,
        out_shape=out_shape,
        compiler_params=_params("parallel"),
        name="ffn_down_final" if final else "ffn_down",
    )(a, w, x, g)


def _rope_pair(g128, c2):
    prod = g128 * c2
    both = prod + pltpu.roll(prod, LANES // 2, axis=1)
    lane = lax.broadcasted_iota(jnp.int32, both.shape, 1)
    return jnp.where(lane < MLA_ROPE, both, 0.0)


def _mla_down_kernel(h_ref, w_ref, gq_ref, gkv_ref, c2_ref, cq_ref, ckv_ref, kr_ref, *, q_rank, kv_rank):
    c = _dot(h_ref[...], w_ref[...])
    cq_ref[...] = _rmsnorm_f32(c[:, :q_rank], gq_ref[...]).astype(BF16)
    ckv_ref[...] = _rmsnorm_f32(c[:, q_rank:q_rank + kv_rank], gkv_ref[...]).astype(BF16)
    kr_ref[...] = _rope_pair(c[:, q_rank + kv_rank:], c2_ref[...]).astype(BF16)


def mla_down(h, w, gq, gkv, c2, *, tm, seq_len):
    n, d = h.shape
    q_rank, kv_rank = gq.shape[1], gkv.shape[1]
    width = w.shape[1]
    assert width == q_rank + kv_rank + LANES
    pos_blocks = seq_len // tm
    return pl.pallas_call(
        functools.partial(_mla_down_kernel, q_rank=q_rank, kv_rank=kv_rank),
        grid=(n // tm,),
        in_specs=[
            pl.BlockSpec((tm, d), lambda i: (i, 0)),
            _resident((d, width), lambda i: (0, 0)),
            pl.BlockSpec((1, q_rank), lambda i: (0, 0)),
            pl.BlockSpec((1, kv_rank), lambda i: (0, 0)),
            pl.BlockSpec((tm, LANES), lambda i: (i % pos_blocks, 0)),
        ],
        out_specs=[
            pl.BlockSpec((tm, q_rank), lambda i: (i, 0)),
            pl.BlockSpec((tm, kv_rank), lambda i: (i, 0)),
            pl.BlockSpec((tm, LANES), lambda i: (i, 0)),
        ],
        out_shape=[
            jax.ShapeDtypeStruct((n, q_rank), BF16),
            jax.ShapeDtypeStruct((n, kv_rank), BF16),
            jax.ShapeDtypeStruct((n, LANES), BF16),
        ],
        compiler_params=_params("parallel"),
        name="mla_down",
    )(h, w, gq, gkv, c2)


def _mla_q_kernel(cq_ref, w_ref, c2_ref, q_ref, *, scale):
    acc = _dot(cq_ref[...], w_ref[...])
    c2 = c2_ref[...]
    for hh in range(q_ref.shape[0]):
        base = hh * MLA_QK
        q_ref[hh, :, :MLA_NOPE] = (acc[:, base:base + MLA_NOPE] * scale).astype(BF16)
        rope = _rope_pair(acc[:, base + MLA_NOPE:base + MLA_QK], c2)
        q_ref[hh, :, MLA_NOPE:] = (rope * scale).astype(BF16)


def mla_q_proj(cq, w, c2, *, tm, heads_per_step, seq_len, scale):
    n, q_rank = cq.shape
    heads = w.shape[1] // MLA_QK
    tn = heads_per_step * MLA_QK
    pos_blocks = seq_len // tm
    return pl.pallas_call(
        functools.partial(_mla_q_kernel, scale=scale),
        grid=(n // tm, heads // heads_per_step),
        in_specs=[
            pl.BlockSpec((tm, q_rank), lambda i, j: (i, 0)),
            pl.BlockSpec((q_rank, tn), lambda i, j: (0, j)),
            pl.BlockSpec((tm, LANES), lambda i, j: (i % pos_blocks, 0)),
        ],
        out_specs=pl.BlockSpec((heads_per_step, tm, MLA_QK), lambda i, j: (j, i, 0)),
        out_shape=jax.ShapeDtypeStruct((heads, n, MLA_QK), BF16),
        compiler_params=_params("parallel", "parallel"),
        name="mla_q_proj",
    )(cq, w, c2)


def _mla_kv_kernel(ckv_ref, kr_ref, wk_ref, wvt_ref, k_ref, vt_ref):
    ckv = ckv_ref[...]
    kn = _dot(ckv, wk_ref[...])
    vt = _dot_nt(wvt_ref[...], ckv)
    kr = kr_ref[...]
    tm = ckv.shape[0]
    row = lax.broadcasted_iota(jnp.int32, (BF16_SUBLANES, tm), 0)
    ones_rows = jnp.where(row == 0, 1.0, 0.0).astype(BF16)
    for hh in range(k_ref.shape[0]):
        k_ref[hh, :, :MLA_NOPE] = kn[:, hh * MLA_NOPE:(hh + 1) * MLA_NOPE].astype(BF16)
        k_ref[hh, :, MLA_NOPE:] = kr
        vt_ref[hh, :MLA_V, :] = vt[hh * MLA_V:(hh + 1) * MLA_V, :].astype(BF16)
        vt_ref[hh, MLA_V:, :] = ones_rows


def mla_kv_proj(ckv, kr, wk, wvt, *, tm, heads_per_step):
    n, kv_rank = ckv.shape
    heads = wk.shape[1] // MLA_NOPE
    hps = heads_per_step
    return pl.pallas_call(
        _mla_kv_kernel,
        grid=(n // tm, heads // hps),
        in_specs=[
            pl.BlockSpec((tm, kv_rank), lambda i, j: (i, 0)),
            pl.BlockSpec((tm, LANES), lambda i, j: (i, 0)),
            pl.BlockSpec((kv_rank, hps * MLA_NOPE), lambda i, j: (0, j)),
            pl.BlockSpec((hps * MLA_V, kv_rank), lambda i, j: (j, 0)),
        ],
        out_specs=[
            pl.BlockSpec((hps, tm, MLA_QK), lambda i, j: (j, i, 0)),
            pl.BlockSpec((hps, MLA_VT_ROWS, tm), lambda i, j: (j, 0, i)),
        ],
        out_shape=[
            jax.ShapeDtypeStruct((heads, n, MLA_QK), BF16),
            jax.ShapeDtypeStruct((heads, MLA_VT_ROWS, n), BF16),
        ],
        compiler_params=_params("parallel", "parallel"),
        name="mla_kv_proj",
    )(ckv, kr, wk, wvt)


def _mla_attn_kernel(q_ref, k_ref, vt_ref, o_ref, m_ref, acc_ref, s0_ref, s1_ref, mt0_ref, mt1_ref, *, tk, unroll):
    n_kt = k_ref.shape[1] // tk
    assert unroll % 2 == 0
    s_refs, mt_refs = (s0_ref, s1_ref), (mt0_ref, mt1_ref)
    m_ref[...] = jnp.full(m_ref.shape, MASK_VALUE, F32)
    acc_ref[...] = jnp.zeros(acc_ref.shape, F32)

    tq = q_ref.shape[1]
    chunks = [pl.ds(c * MLA_QCHUNK, MLA_QCHUNK) for c in range(tq // MLA_QCHUNK)]

    def scores(kt, slot, cols):
        start = pl.multiple_of(kt * tk, tk)
        s = _dot_nt(k_ref[0, pl.ds(start, tk), :], q_ref[0, cols, :])
        s_refs[slot][:, cols] = s
        mt_refs[slot][:, cols] = jnp.max(s, axis=0, keepdims=True)

    def probs(slot, cols):
        m_prev = m_ref[:, cols]
        m_new = jnp.maximum(m_prev, mt_refs[slot][:, cols])
        m_ref[:, cols] = m_new
        return jnp.exp2(s_refs[slot][:, cols] - m_new).astype(BF16), jnp.exp2(m_prev - m_new)

    def accumulate(kt, cols, p, alpha):
        start = pl.multiple_of(kt * tk, tk)
        acc_ref[:, cols] = acc_ref[:, cols] * alpha + _dot(vt_ref[0, :, pl.ds(start, tk)], p)

    def step(kt, slot, with_next):
        for cols in chunks:
            p, alpha = probs(slot, cols)
            if with_next:
                scores(kt + 1, 1 - slot, cols)
            accumulate(kt, cols, p, alpha)

    for cols in chunks:
        scores(0, 0, cols)
    n_loop = (n_kt - 1) // unroll

    def body(j, carry):
        for u in range(unroll):
            step(unroll * j + u, u % 2, True)
        return carry

    lax.fori_loop(0, n_loop, body, 0)
    for kt in range(n_loop * unroll, n_kt):
        step(kt, kt % 2, kt + 1 < n_kt)
    acc = acc_ref[...]
    o_t = acc[:MLA_V, :] / acc[MLA_V:MLA_V + 1, :]
    o_ref[0] = o_t.T.astype(BF16)


def mla_attention(q_hm, k_hm, vt_hm, *, n_seq, seq_len, tq, tk, unroll):
    heads, n, _ = q_hm.shape
    nq = seq_len // tq
    return pl.pallas_call(
        functools.partial(_mla_attn_kernel, tk=tk, unroll=unroll),
        grid=(n_seq, heads, nq),
        in_specs=[
            pl.BlockSpec((1, tq, MLA_QK), lambda s, h, i: (h, s * nq + i, 0)),
            pl.BlockSpec((1, seq_len, MLA_QK), lambda s, h, i: (h, s, 0)),
            pl.BlockSpec((1, MLA_VT_ROWS, seq_len), lambda s, h, i: (h, 0, s)),
        ],
        out_specs=pl.BlockSpec((1, tq, MLA_V), lambda s, h, i: (h, s * nq + i, 0)),
        out_shape=jax.ShapeDtypeStruct((heads, n, MLA_V), BF16),
        scratch_shapes=[pltpu.VMEM((1, tq), F32), pltpu.VMEM((MLA_VT_ROWS, tq), F32),
                        pltpu.VMEM((tk, tq), F32), pltpu.VMEM((tk, tq), F32),
                        pltpu.VMEM((1, tq), F32), pltpu.VMEM((1, tq), F32)],
        compiler_params=_params("parallel", "parallel", "arbitrary"),
        name="mla_attention",
    )(q_hm, k_hm, vt_hm)


def rope_table(seq_len):
    inv_freq = ROPE_THETA ** (-jnp.arange(0, MLA_ROPE, 2, dtype=F32) / MLA_ROPE)
    ang = jnp.arange(seq_len, dtype=F32)[:, None] * inv_freq[None, :]
    cos, sin = jnp.cos(ang), jnp.sin(ang)
    return jnp.concatenate([cos, cos, -sin, sin], axis=-1)


def _swap_halves(w):
    half = w.shape[-1] // 2
    return jnp.concatenate([w[..., half:], w[..., :half]], axis=-1)


def trunk(x, seq_len, out_rows, p, tiles):
    n, d = x.shape
    n_seq = n // seq_len
    rows = seq_len // GRID_W
    row2 = lambda v: v.reshape(1, -1)

    w_qkv = p["na_w_qkv"][0].astype(BF16)
    na_dh = d // NA_HEADS
    col_scale = jnp.concatenate([jnp.full((d,), LOG2_E * na_dh ** -0.5, F32), jnp.ones((2 * d,), F32)])
    qkv = qkv_proj(x, row2(p["g_mix"][0]), w_qkv, row2(p["na_b_qkv"][0]), row2(col_scale),
                   tm=tiles["qkv_tm"], tn=tiles["qkv_tn"])
    bias = na_bias_table(p["na_rpb"][0], rows)
    o = na_attention(qkv, bias, n_seq=n_seq, seq_len=seq_len)
    x, h = out_proj(o, p["na_w_o"][0].astype(BF16), x, row2(p["g_ffn"][0]), tm=tiles["tm"])
    a = _ffn_up(h, p, 0, tiles)
    x, h = ffn_down(a, p["ffn_w_down"][0].astype(BF16), x, row2(p["g_mix"][1]), tm=tiles["down_tm"])

    w_kr = p["mla_w_kr"][0]
    w_down = jnp.concatenate([p["mla_w_dq"][0], p["mla_w_dkv"][0], w_kr, _swap_halves(w_kr)], axis=1).astype(BF16)
    c2 = rope_table(seq_len)
    cq, ckv, kr = mla_down(h, w_down, row2(p["mla_g_q"][0]), row2(p["mla_g_kv"][0]), c2,
                           tm=tiles["tm"], seq_len=seq_len)
    q_rank = cq.shape[1]
    w_uq = p["mla_w_uq"][0].reshape(q_rank, MLA_HEADS, MLA_NOPE + MLA_ROPE)
    w_uq = jnp.concatenate([w_uq, _swap_halves(w_uq[..., MLA_NOPE:])], axis=-1)
    w_uq = w_uq.reshape(q_rank, MLA_HEADS * MLA_QK).astype(BF16)
    q_hm = mla_q_proj(cq, w_uq, c2, tm=tiles["tm"], heads_per_step=tiles["q_hps"], seq_len=seq_len,
                      scale=LOG2_E * (MLA_NOPE + MLA_ROPE) ** -0.5)
    kv_rank = ckv.shape[1]
    w_ukv = p["mla_w_ukv"][0].reshape(kv_rank, MLA_HEADS, MLA_NOPE + MLA_V)
    w_uk = w_ukv[..., :MLA_NOPE].reshape(kv_rank, MLA_HEADS * MLA_NOPE).astype(BF16)
    w_uvt = w_ukv[..., MLA_NOPE:].reshape(kv_rank, MLA_HEADS * MLA_V).T.astype(BF16)
    k_hm, vt_hm = mla_kv_proj(ckv, kr, w_uk, w_uvt, tm=tiles["tm"], heads_per_step=tiles["kv_hps"])
    o = mla_attention(q_hm, k_hm, vt_hm, n_seq=n_seq, seq_len=seq_len, tq=tiles["mla_tq"], tk=tiles["mla_tk"],
                      unroll=tiles["mla_unroll"])
    x, h = out_proj(o, p["mla_w_o"][0].astype(BF16), x, row2(p["g_ffn"][1]), tm=tiles["tm"])
    a = _ffn_up(h, p, 1, tiles)
    w_down = p["ffn_w_down"][1].astype(BF16)
    return [ffn_down(a, w_down, x, row2(p["g_final"]), tm=tiles["down_tm"], final=True, row_start=r0, n_rows=nr)
            for r0, nr in out_rows]


def _ffn_up(h, p, layer, tiles):
    return ffn_up(h, p["ffn_w_gate"][layer].astype(BF16), p["ffn_w_up"][layer].astype(BF16),
                  tm=tiles["ffn_tm"], tn=tiles["ffn_tn"])


TILES = dict(tm=512, qkv_tm=1024, qkv_tn=1024, ffn_tm=1024, ffn_tn=512, down_tm=256, q_hps=16, kv_hps=16,
             mla_tq=2048, mla_tk=1024, mla_unroll=4)


def kernel(x_prompt, x_sample, g_mix, g_ffn, g_final, na_w_qkv, na_b_qkv, na_rpb, na_w_o, mla_w_dq, mla_g_q,
           mla_w_uq, mla_w_dkv, mla_g_kv, mla_w_kr, mla_w_ukv, mla_w_o, ffn_w_gate, ffn_w_up, ffn_w_down):
    p = dict(g_mix=g_mix, g_ffn=g_ffn, g_final=g_final, na_w_qkv=na_w_qkv, na_b_qkv=na_b_qkv, na_rpb=na_rpb,
             na_w_o=na_w_o, mla_w_dq=mla_w_dq, mla_g_q=mla_g_q, mla_w_uq=mla_w_uq, mla_w_dkv=mla_w_dkv,
             mla_g_kv=mla_g_kv, mla_w_kr=mla_w_kr, mla_w_ukv=mla_w_ukv, mla_w_o=mla_w_o,
             ffn_w_gate=ffn_w_gate, ffn_w_up=ffn_w_up, ffn_w_down=ffn_w_down)
    bp, seq_len, d = x_prompt.shape
    bs = x_sample.shape[0]
    assert x_sample.shape[1:] == (seq_len, d)
    x = jnp.concatenate([x_prompt.reshape(bp * seq_len, d), x_sample.reshape(bs * seq_len, d)], axis=0)
    out_rows = [(0, bp * seq_len), (bp * seq_len, bs * seq_len)]
    y_prompt, y_sample = trunk(x, seq_len, out_rows, p, TILES)
    return (y_prompt.reshape(bp, seq_len, d), y_sample.reshape(bs, seq_len, d))
```

```python
import functools

import jax
import jax.numpy as jnp
from jax import lax
from jax.experimental import pallas as pl
from jax.experimental.pallas import tpu as pltpu

GRID_W = 64
NA_HEADS = 16
NA_KR = 8
NA_KC = 16
MLA_HEADS = 16
MLA_NOPE = 128
MLA_ROPE = 64
MLA_V = 128
ROPE_THETA = 10000.0
EPS = 1e-6
LOG2_E = 1.4426950408889634

LANES = 128
BF16_SUBLANES = 16
VMEM_LIMIT_BYTES = 56 * 1024 * 1024

NA_QROWS = 4
NA_KROWS = 12
NA_KBLK_ROWS = 4
NA_UNITS = 16
NA_WINDOW_BLOCKS = NA_UNITS + NA_KROWS // NA_KBLK_ROWS - 1
MASK_VALUE = -1e30

MLA_QK = 256
MLA_VT_ROWS = MLA_V + BF16_SUBLANES
MLA_QCHUNK = 256

F32 = jnp.float32
BF16 = jnp.bfloat16


def _params(*sem):
    return pltpu.CompilerParams(dimension_semantics=sem, vmem_limit_bytes=VMEM_LIMIT_BYTES)


def _rmsnorm_f32(x, g):
    return x * lax.rsqrt(jnp.mean(x * x, axis=-1, keepdims=True) + EPS) * g


def _dot(a, b):
    return jnp.dot(a, b, preferred_element_type=F32)


def _dot_nt(a, b):
    return lax.dot_general(a, b, (((1,), (1,)), ((), ())), preferred_element_type=F32)


def _two_part_specs(block, n_first_tiles):
    return [pl.BlockSpec(block, lambda i, *_: (jnp.minimum(i, n_first_tiles - 1), 0)),
            pl.BlockSpec(block, lambda i, *_: (jnp.maximum(i - n_first_tiles, 0), 0))]


def _for_owning_part(x_refs, n_first_tiles, fn):
    if len(x_refs) == 1:
        fn(x_refs[0])
        return
    i = pl.program_id(0)
    pl.when(i < n_first_tiles)(lambda: fn(x_refs[0]))
    pl.when(i >= n_first_tiles)(lambda: fn(x_refs[1]))


def _input_norm_kernel(xa_ref, xb_ref, g_ref, h_ref, *, n_first_tiles):
    def write(x_ref):
        h_ref[...] = _rmsnorm_f32(x_ref[...], g_ref[...]).astype(BF16)

    _for_owning_part((xa_ref, xb_ref), n_first_tiles, write)


def input_norm(x_parts, g, *, tm):
    xa, xb = x_parts
    d = xa.shape[1]
    assert xa.shape[0] % tm == 0 and xb.shape[0] % tm == 0
    n_first_tiles = xa.shape[0] // tm
    n = xa.shape[0] + xb.shape[0]
    return pl.pallas_call(
        functools.partial(_input_norm_kernel, n_first_tiles=n_first_tiles),
        grid=(n // tm,),
        in_specs=_two_part_specs((tm, d), n_first_tiles) + [pl.BlockSpec((1, d), lambda i: (0, 0))],
        out_specs=pl.BlockSpec((tm, d), lambda i: (i, 0)),
        out_shape=jax.ShapeDtypeStruct((n, d), BF16),
        compiler_params=_params("parallel"),
        name="input_norm",
    )(xa, xb, g)


def _qkv_kernel(h_ref, w_ref, b_ref, cs_ref, o_ref):
    acc = (_dot(h_ref[...], w_ref[...]) + b_ref[...]) * cs_ref[...]
    for hh in range(o_ref.shape[0]):
        o_ref[hh] = acc[:, hh * LANES:(hh + 1) * LANES].astype(BF16)


def qkv_proj(h, w, b, cs, *, tm, tn):
    n, d = h.shape
    m = w.shape[1]
    return pl.pallas_call(
        _qkv_kernel,
        grid=(n // tm, m // tn),
        in_specs=[
            pl.BlockSpec((tm, d), lambda i, j: (i, 0)),
            pl.BlockSpec((d, tn), lambda i, j: (0, j)),
            pl.BlockSpec((1, tn), lambda i, j: (0, j)),
            pl.BlockSpec((1, tn), lambda i, j: (0, j)),
        ],
        out_specs=pl.BlockSpec((tn // LANES, tm, LANES), lambda i, j: (j, i, 0)),
        out_shape=jax.ShapeDtypeStruct((m // LANES, n, LANES), BF16),
        compiler_params=_params("parallel", "parallel"),
        name="qkv_proj",
    )(h, w, b, cs)


def _na_kernel(q_ref, k_ref, v_ref, bias_ref, o_ref, *, n_units):
    step = pl.program_id(2)
    tq = NA_QROWS * GRID_W
    tk = NA_KROWS * GRID_W
    window_base = _na_window_base(step, n_units)
    for u in range(NA_UNITS):
        unit = NA_UNITS * step + u
        start = pl.multiple_of((_na_key_base(unit, n_units) - window_base) * tq, tq)
        k = k_ref[0, pl.ds(start, tk), :]
        v = v_ref[0, pl.ds(start, tk), :]
        cls = jnp.where(unit == 0, 0, jnp.where(unit == n_units - 1, 2, 1))
        rows = pl.ds(u * tq, tq)
        s = _dot_nt(q_ref[0, rows, :], k) + bias_ref[cls, 0]
        m = jnp.max(s, axis=-1, keepdims=True)
        p = jnp.exp2(s - m)
        l = jnp.sum(p, axis=-1, keepdims=True)
        o = _dot(p.astype(BF16), v)
        o_ref[0, rows, :] = (o / l).astype(BF16)


def _na_key_base(unit, n_units):
    return jnp.clip(unit - 1, 0, n_units - NA_KROWS // NA_KBLK_ROWS)


def _na_window_base(step, n_units):
    return jnp.clip(NA_UNITS * step - 1, 0, n_units - NA_WINDOW_BLOCKS)


def na_bias_table(rpb, rows):
    heads = rpb.shape[0]
    nb = rows // NA_QROWS
    cols = jnp.arange(GRID_W)
    c_start = jnp.clip(cols - NA_KC // 2, 0, GRID_W - NA_KC)
    cvalid = (cols[None, :] >= c_start[:, None]) & (cols[None, :] < c_start[:, None] + NA_KC)
    coff = jnp.clip(cols[None, :] - cols[:, None] + (NA_KC - 1), 0, 2 * NA_KC - 2)
    a = rpb[:, :, coff]
    tables = []
    for b in (0, 1, nb - 1):
        r = b * NA_QROWS + jnp.arange(NA_QROWS)
        r_start = jnp.clip(r - NA_KR // 2, 0, rows - NA_KR)
        kb = NA_KBLK_ROWS * min(max(b - 1, 0), nb - NA_KROWS // NA_KBLK_ROWS)
        krow = kb + jnp.arange(NA_KROWS)
        rvalid = (krow[None, :] >= r_start[:, None]) & (krow[None, :] < r_start[:, None] + NA_KR)
        roff = jnp.clip(krow[None, :] - r[:, None] + (NA_KR - 1), 0, 2 * NA_KR - 2)
        t = a[:, roff]
        valid = rvalid[:, :, None, None] & cvalid[None, None]
        t = jnp.where(valid[None], t * LOG2_E, MASK_VALUE)
        t = t.transpose(0, 1, 3, 2, 4).reshape(heads, NA_QROWS * GRID_W, NA_KROWS * GRID_W)
        tables.append(t)
    return jnp.stack(tables).astype(F32)


def na_attention(qkv_hm, bias, *, n_seq, seq_len):
    heads = NA_HEADS
    rows = seq_len // GRID_W
    assert NA_QROWS == NA_KBLK_ROWS and rows % (NA_QROWS * NA_UNITS) == 0
    n_units = rows // NA_QROWS
    assert n_units >= NA_WINDOW_BLOCKS
    nb = n_units // NA_UNITS
    tq = NA_QROWS * GRID_W
    n = n_seq * seq_len

    def q_map(h, s, b):
        return (h, s * nb + b, 0)

    def kv_map(which):
        def f(h, s, b):
            return (which * heads + h, (s * n_units + _na_window_base(b, n_units)) * tq, 0)
        return f

    kv_block = (pl.Element(1), pl.Element(NA_WINDOW_BLOCKS * tq), pl.Element(LANES))
    return pl.pallas_call(
        functools.partial(_na_kernel, n_units=n_units),
        grid=(heads, n_seq, nb),
        in_specs=[
            pl.BlockSpec((1, NA_UNITS * tq, LANES), q_map),
            pl.BlockSpec(kv_block, kv_map(1)),
            pl.BlockSpec(kv_block, kv_map(2)),
            pl.BlockSpec((3, 1, tq, NA_KROWS * GRID_W), lambda h, s, b: (0, h, 0, 0)),
        ],
        out_specs=pl.BlockSpec((1, NA_UNITS * tq, LANES), q_map),
        out_shape=jax.ShapeDtypeStruct((heads, n, LANES), BF16),
        compiler_params=_params("parallel", "parallel", "arbitrary"),
        name="na_attention",
    )(qkv_hm, qkv_hm, qkv_hm, bias)


def _resident(block_shape, index_map):
    return pl.BlockSpec(block_shape, index_map, pipeline_mode=pl.Buffered(1))


def _oproj_kernel(o_ref, w_ref, *rest, n_first_tiles):
    *x_refs, g_ref, y_ref, h_ref = rest
    o = jnp.concatenate([o_ref[h] for h in range(o_ref.shape[0])], axis=-1)
    if len(x_refs) == 1:
        x = x_refs[0][...]
    else:
        x = lax.select(pl.program_id(0) < n_first_tiles, x_refs[0][...], x_refs[1][...])
    y = x + _dot(o, w_ref[...])
    y_ref[...] = y
    h_ref[...] = _rmsnorm_f32(y, g_ref[...]).astype(BF16)


def out_proj(o_hm, w, x_parts, g, *, tm):
    heads, n, dh = o_hm.shape
    d = w.shape[1]
    if len(x_parts) == 1:
        n_first_tiles, x_specs = 0, [pl.BlockSpec((tm, d), lambda i: (i, 0))]
    else:
        assert all(x.shape[0] % tm == 0 for x in x_parts)
        n_first_tiles = x_parts[0].shape[0] // tm
        x_specs = _two_part_specs((tm, d), n_first_tiles)
    return pl.pallas_call(
        functools.partial(_oproj_kernel, n_first_tiles=n_first_tiles),
        grid=(n // tm,),
        in_specs=[
            pl.BlockSpec((heads, tm, dh), lambda i: (0, i, 0)),
            _resident((heads * dh, d), lambda i: (0, 0)),
            *x_specs,
            pl.BlockSpec((1, d), lambda i: (0, 0)),
        ],
        out_specs=[pl.BlockSpec((tm, d), lambda i: (i, 0)), pl.BlockSpec((tm, d), lambda i: (i, 0))],
        out_shape=[jax.ShapeDtypeStruct((n, d), F32), jax.ShapeDtypeStruct((n, d), BF16)],
        compiler_params=_params("parallel"),
        name="out_proj",
    )(o_hm, w, *x_parts, g)


def _ffn_up_kernel(h_ref, wg_ref, wu_ref, a_ref):
    h = h_ref[...]
    gate = _dot(h, wg_ref[...])
    up = _dot(h, wu_ref[...])
    a_ref[...] = (gate * jax.nn.sigmoid(gate) * up).astype(BF16)


def ffn_up(h, wg, wu, *, tm, tn):
    n, d = h.shape
    f = wg.shape[1]
    return pl.pallas_call(
        _ffn_up_kernel,
        grid=(n // tm, f // tn),
        in_specs=[
            pl.BlockSpec((tm, d), lambda i, j: (i, 0)),
            pl.BlockSpec((d, tn), lambda i, j: (0, j)),
            pl.BlockSpec((d, tn), lambda i, j: (0, j)),
        ],
        out_specs=pl.BlockSpec((tm, tn), lambda i, j: (i, j)),
        out_shape=jax.ShapeDtypeStruct((n, f), BF16),
        compiler_params=_params("parallel", "parallel"),
        name="ffn_up",
    )(h, wg, wu)


def _ffn_down_kernel(a_ref, w_ref, x_ref, g_ref, y_ref, h_ref):
    y = x_ref[...] + _dot(a_ref[...], w_ref[...])
    y_ref[...] = y
    h_ref[...] = _rmsnorm_f32(y, g_ref[...]).astype(BF16)


def _ffn_down_final_kernel(a_ref, w_ref, x_ref, g_ref, y_ref):
    y_ref[...] = _rmsnorm_f32(x_ref[...] + _dot(a_ref[...], w_ref[...]), g_ref[...])


def ffn_down(a, w, x, g, *, tm, final=False, row_start=0, n_rows=None):
    f = a.shape[1]
    d = w.shape[1]
    n_rows = a.shape[0] if n_rows is None else n_rows
    off = row_start // tm
    assert row_start % tm == 0 and n_rows % tm == 0
    in_specs = [
        pl.BlockSpec((tm, f), lambda i: (i + off, 0)),
        _resident((f, d), lambda i: (0, 0)),
        pl.BlockSpec((tm, d), lambda i: (i + off, 0)),
        pl.BlockSpec((1, d), lambda i: (0, 0)),
    ]
    row_block = pl.BlockSpec((tm, d), lambda i: (i, 0))
    if final:
        body, out_specs, out_shape = _ffn_down_final_kernel, row_block, jax.ShapeDtypeStruct((n_rows, d), F32)
    else:
        body, out_specs = _ffn_down_kernel, [row_block, row_block]
        out_shape = [jax.ShapeDtypeStruct((n_rows, d), F32), jax.ShapeDtypeStruct((n_rows, d), BF16)]
    return pl.pallas_call(
        body,
        grid=(n_rows // tm,),
        in_specs=in_specs,
        out_specs=out_specs,
        out_shape=out_shape,
        compiler_params=_params("parallel"),
        name="ffn_down_final" if final else "ffn_down",
    )(a, w, x, g)


def _rope_pair(g128, c2):
    prod = g128 * c2
    both = prod + pltpu.roll(prod, LANES // 2, axis=1)
    lane = lax.broadcasted_iota(jnp.int32, both.shape, 1)
    return jnp.where(lane < MLA_ROPE, both, 0.0)


def _mla_down_kernel(h_ref, w_ref, gq_ref, gkv_ref, c2_ref, cq_ref, ckv_ref, kr_ref, *, q_rank, kv_rank):
    c = _dot(h_ref[...], w_ref[...])
    cq_ref[...] = _rmsnorm_f32(c[:, :q_rank], gq_ref[...]).astype(BF16)
    ckv_ref[...] = _rmsnorm_f32(c[:, q_rank:q_rank + kv_rank], gkv_ref[...]).astype(BF16)
    kr_ref[...] = _rope_pair(c[:, q_rank + kv_rank:], c2_ref[...]).astype(BF16)


def mla_down(h, w, gq, gkv, c2, *, tm, seq_len):
    n, d = h.shape
    q_rank, kv_rank = gq.shape[1], gkv.shape[1]
    width = w.shape[1]
    assert width == q_rank + kv_rank + LANES
    pos_blocks = seq_len // tm
    return pl.pallas_call(
        functools.partial(_mla_down_kernel, q_rank=q_rank, kv_rank=kv_rank),
        grid=(n // tm,),
        in_specs=[
            pl.BlockSpec((tm, d), lambda i: (i, 0)),
            _resident((d, width), lambda i: (0, 0)),
            pl.BlockSpec((1, q_rank), lambda i: (0, 0)),
            pl.BlockSpec((1, kv_rank), lambda i: (0, 0)),
            pl.BlockSpec((tm, LANES), lambda i: (i % pos_blocks, 0)),
        ],
        out_specs=[
            pl.BlockSpec((tm, q_rank), lambda i: (i, 0)),
            pl.BlockSpec((tm, kv_rank), lambda i: (i, 0)),
            pl.BlockSpec((tm, LANES), lambda i: (i, 0)),
        ],
        out_shape=[
            jax.ShapeDtypeStruct((n, q_rank), BF16),
            jax.ShapeDtypeStruct((n, kv_rank), BF16),
            jax.ShapeDtypeStruct((n, LANES), BF16),
        ],
        compiler_params=_params("parallel"),
        name="mla_down",
    )(h, w, gq, gkv, c2)


def _mla_q_kernel(cq_ref, w_ref, c2_ref, q_ref, *, scale):
    acc = _dot(cq_ref[...], w_ref[...])
    c2 = c2_ref[...]
    for hh in range(q_ref.shape[0]):
        base = hh * MLA_QK
        q_ref[hh, :, :MLA_NOPE] = (acc[:, base:base + MLA_NOPE] * scale).astype(BF16)
        rope = _rope_pair(acc[:, base + MLA_NOPE:base + MLA_QK], c2)
        q_ref[hh, :, MLA_NOPE:] = (rope * scale).astype(BF16)


def mla_q_proj(cq, w, c2, *, tm, heads_per_step, seq_len, scale):
    n, q_rank = cq.shape
    heads = w.shape[1] // MLA_QK
    tn = heads_per_step * MLA_QK
    pos_blocks = seq_len // tm
    return pl.pallas_call(
        functools.partial(_mla_q_kernel, scale=scale),
        grid=(n // tm, heads // heads_per_step),
        in_specs=[
            pl.BlockSpec((tm, q_rank), lambda i, j: (i, 0)),
            pl.BlockSpec((q_rank, tn), lambda i, j: (0, j)),
            pl.BlockSpec((tm, LANES), lambda i, j: (i % pos_blocks, 0)),
        ],
        out_specs=pl.BlockSpec((heads_per_step, tm, MLA_QK), lambda i, j: (j, i, 0)),
        out_shape=jax.ShapeDtypeStruct((heads, n, MLA_QK), BF16),
        compiler_params=_params("parallel", "parallel"),
        name="mla_q_proj",
    )(cq, w, c2)


def _mla_kv_kernel(ckv_ref, kr_ref, wk_ref, wvt_ref, k_ref, vt_ref):
    ckv = ckv_ref[...]
    kn = _dot(ckv, wk_ref[...])
    vt = _dot_nt(wvt_ref[...], ckv)
    kr = kr_ref[...]
    tm = ckv.shape[0]
    row = lax.broadcasted_iota(jnp.int32, (BF16_SUBLANES, tm), 0)
    ones_rows = jnp.where(row == 0, 1.0, 0.0).astype(BF16)
    for hh in range(k_ref.shape[0]):
        k_ref[hh, :, :MLA_NOPE] = kn[:, hh * MLA_NOPE:(hh + 1) * MLA_NOPE].astype(BF16)
        k_ref[hh, :, MLA_NOPE:] = kr
        vt_ref[hh, :MLA_V, :] = vt[hh * MLA_V:(hh + 1) * MLA_V, :].astype(BF16)
        vt_ref[hh, MLA_V:, :] = ones_rows


def mla_kv_proj(ckv, kr, wk, wvt, *, tm, heads_per_step):
    n, kv_rank = ckv.shape
    heads = wk.shape[1] // MLA_NOPE
    hps = heads_per_step
    return pl.pallas_call(
        _mla_kv_kernel,
        grid=(n // tm, heads // hps),
        in_specs=[
            pl.BlockSpec((tm, kv_rank), lambda i, j: (i, 0)),
            pl.BlockSpec((tm, LANES), lambda i, j: (i, 0)),
            pl.BlockSpec((kv_rank, hps * MLA_NOPE), lambda i, j: (0, j)),
            pl.BlockSpec((hps * MLA_V, kv_rank), lambda i, j: (j, 0)),
        ],
        out_specs=[
            pl.BlockSpec((hps, tm, MLA_QK), lambda i, j: (j, i, 0)),
            pl.BlockSpec((hps, MLA_VT_ROWS, tm), lambda i, j: (j, 0, i)),
        ],
        out_shape=[
            jax.ShapeDtypeStruct((heads, n, MLA_QK), BF16),
            jax.ShapeDtypeStruct((heads, MLA_VT_ROWS, n), BF16),
        ],
        compiler_params=_params("parallel", "parallel"),
        name="mla_kv_proj",
    )(ckv, kr, wk, wvt)


def _mla_attn_kernel(q_ref, k_ref, vt_ref, o_ref, m_ref, acc_ref, s0_ref, s1_ref, mt0_ref, mt1_ref, *, tk, unroll):
    n_kt = k_ref.shape[1] // tk
    assert unroll % 2 == 0
    s_refs, mt_refs = (s0_ref, s1_ref), (mt0_ref, mt1_ref)
    m_ref[...] = jnp.full(m_ref.shape, MASK_VALUE, F32)
    acc_ref[...] = jnp.zeros(acc_ref.shape, F32)

    tq = q_ref.shape[1]
    chunks = [pl.ds(c * MLA_QCHUNK, MLA_QCHUNK) for c in range(tq // MLA_QCHUNK)]

    def scores(kt, slot, cols):
        start = pl.multiple_of(kt * tk, tk)
        s = _dot_nt(k_ref[0, pl.ds(start, tk), :], q_ref[0, cols, :])
        s_refs[slot][:, cols] = s
        mt_refs[slot][:, cols] = jnp.max(s, axis=0, keepdims=True)

    def probs(slot, cols):
        m_prev = m_ref[:, cols]
        m_new = jnp.maximum(m_prev, mt_refs[slot][:, cols])
        m_ref[:, cols] = m_new
        return jnp.exp2(s_refs[slot][:, cols] - m_new).astype(BF16), jnp.exp2(m_prev - m_new)

    def accumulate(kt, cols, p, alpha):
        start = pl.multiple_of(kt * tk, tk)
        acc_ref[:, cols] = acc_ref[:, cols] * alpha + _dot(vt_ref[0, :, pl.ds(start, tk)], p)

    def step(kt, slot, with_next):
        for cols in chunks:
            p, alpha = probs(slot, cols)
            if with_next:
                scores(kt + 1, 1 - slot, cols)
            accumulate(kt, cols, p, alpha)

    for cols in chunks:
        scores(0, 0, cols)
    n_loop = (n_kt - 1) // unroll

    def body(j, carry):
        for u in range(unroll):
            step(unroll * j + u, u % 2, True)
        return carry

    lax.fori_loop(0, n_loop, body, 0)
    for kt in range(n_loop * unroll, n_kt):
        step(kt, kt % 2, kt + 1 < n_kt)
    acc = acc_ref[...]
    o_t = acc[:MLA_V, :] / acc[MLA_V:MLA_V + 1, :]
    o_ref[0] = o_t.T.astype(BF16)


def mla_attention(q_hm, k_hm, vt_hm, *, n_seq, seq_len, tq, tk, unroll):
    heads, n, _ = q_hm.shape
    nq = seq_len // tq
    return pl.pallas_call(
        functools.partial(_mla_attn_kernel, tk=tk, unroll=unroll),
        grid=(n_seq, heads, nq),
        in_specs=[
            pl.BlockSpec((1, tq, MLA_QK), lambda s, h, i: (h, s * nq + i, 0)),
            pl.BlockSpec((1, seq_len, MLA_QK), lambda s, h, i: (h, s, 0)),
            pl.BlockSpec((1, MLA_VT_ROWS, seq_len), lambda s, h, i: (h, 0, s)),
        ],
        out_specs=pl.BlockSpec((1, tq, MLA_V), lambda s, h, i: (h, s * nq + i, 0)),
        out_shape=jax.ShapeDtypeStruct((heads, n, MLA_V), BF16),
        scratch_shapes=[pltpu.VMEM((1, tq), F32), pltpu.VMEM((MLA_VT_ROWS, tq), F32),
                        pltpu.VMEM((tk, tq), F32), pltpu.VMEM((tk, tq), F32),
                        pltpu.VMEM((1, tq), F32), pltpu.VMEM((1, tq), F32)],
        compiler_params=_params("parallel", "parallel", "arbitrary"),
        name="mla_attention",
    )(q_hm, k_hm, vt_hm)


def rope_table(seq_len):
    inv_freq = ROPE_THETA ** (-jnp.arange(0, MLA_ROPE, 2, dtype=F32) / MLA_ROPE)
    ang = jnp.arange(seq_len, dtype=F32)[:, None] * inv_freq[None, :]
    cos, sin = jnp.cos(ang), jnp.sin(ang)
    return jnp.concatenate([cos, cos, -sin, sin], axis=-1)


def _swap_halves(w):
    half = w.shape[-1] // 2
    return jnp.concatenate([w[..., half:], w[..., :half]], axis=-1)


def trunk(x_parts, seq_len, p, tiles):
    d = x_parts[0].shape[1]
    n = sum(x.shape[0] for x in x_parts)
    n_seq = n // seq_len
    rows = seq_len // GRID_W
    row2 = lambda v: v.reshape(1, -1)

    w_qkv = p["na_w_qkv"][0].astype(BF16)
    na_dh = d // NA_HEADS
    col_scale = jnp.concatenate([jnp.full((d,), LOG2_E * na_dh ** -0.5, F32), jnp.ones((2 * d,), F32)])
    h = input_norm(x_parts, row2(p["g_mix"][0]), tm=tiles["tm"])
    qkv = qkv_proj(h, w_qkv, row2(p["na_b_qkv"][0]), row2(col_scale), tm=tiles["qkv_tm"], tn=tiles["qkv_tn"])
    bias = na_bias_table(p["na_rpb"][0], rows)
    o = na_attention(qkv, bias, n_seq=n_seq, seq_len=seq_len)
    x, h = out_proj(o, p["na_w_o"][0].astype(BF16), x_parts, row2(p["g_ffn"][0]), tm=tiles["tm"])
    a = _ffn_up(h, p, 0, tiles)
    x, h = ffn_down(a, p["ffn_w_down"][0].astype(BF16), x, row2(p["g_mix"][1]), tm=tiles["down_tm"])

    w_kr = p["mla_w_kr"][0]
    w_down = jnp.concatenate([p["mla_w_dq"][0], p["mla_w_dkv"][0], w_kr, _swap_halves(w_kr)], axis=1).astype(BF16)
    c2 = rope_table(seq_len)
    cq, ckv, kr = mla_down(h, w_down, row2(p["mla_g_q"][0]), row2(p["mla_g_kv"][0]), c2,
                           tm=tiles["tm"], seq_len=seq_len)
    q_rank = cq.shape[1]
    w_uq = p["mla_w_uq"][0].reshape(q_rank, MLA_HEADS, MLA_NOPE + MLA_ROPE)
    w_uq = jnp.concatenate([w_uq, _swap_halves(w_uq[..., MLA_NOPE:])], axis=-1)
    w_uq = w_uq.reshape(q_rank, MLA_HEADS * MLA_QK).astype(BF16)
    q_hm = mla_q_proj(cq, w_uq, c2, tm=tiles["tm"], heads_per_step=tiles["q_hps"], seq_len=seq_len,
                      scale=LOG2_E * (MLA_NOPE + MLA_ROPE) ** -0.5)
    kv_rank = ckv.shape[1]
    w_ukv = p["mla_w_ukv"][0].reshape(kv_rank, MLA_HEADS, MLA_NOPE + MLA_V)
    w_uk = w_ukv[..., :MLA_NOPE].reshape(kv_rank, MLA_HEADS * MLA_NOPE).astype(BF16)
    w_uvt = w_ukv[..., MLA_NOPE:].reshape(kv_rank, MLA_HEADS * MLA_V).T.astype(BF16)
    k_hm, vt_hm = mla_kv_proj(ckv, kr, w_uk, w_uvt, tm=tiles["tm"], heads_per_step=tiles["kv_hps"])
    o = mla_attention(q_hm, k_hm, vt_hm, n_seq=n_seq, seq_len=seq_len, tq=tiles["mla_tq"], tk=tiles["mla_tk"],
                      unroll=tiles["mla_unroll"])
    x, h = out_proj(o, p["mla_w_o"][0].astype(BF16), [x], row2(p["g_ffn"][1]), tm=tiles["tm"])
    a = _ffn_up(h, p, 1, tiles)
    w_down = p["ffn_w_down"][1].astype(BF16)
    outs, row_start = [], 0
    for part in x_parts:
        outs.append(ffn_down(a, w_down, x, row2(p["g_final"]), tm=tiles["down_tm"], final=True,
                             row_start=row_start, n_rows=part.shape[0]))
        row_start += part.shape[0]
    return outs


def _ffn_up(h, p, layer, tiles):
    return ffn_up(h, p["ffn_w_gate"][layer].astype(BF16), p["ffn_w_up"][layer].astype(BF16),
                  tm=tiles["ffn_tm"], tn=tiles["ffn_tn"])


TILES = dict(tm=512, qkv_tm=1024, qkv_tn=1024, ffn_tm=1024, ffn_tn=512, down_tm=256, q_hps=16, kv_hps=16,
             mla_tq=2048, mla_tk=1024, mla_unroll=4)


def kernel(x_prompt, x_sample, g_mix, g_ffn, g_final, na_w_qkv, na_b_qkv, na_rpb, na_w_o, mla_w_dq, mla_g_q,
           mla_w_uq, mla_w_dkv, mla_g_kv, mla_w_kr, mla_w_ukv, mla_w_o, ffn_w_gate, ffn_w_up, ffn_w_down):
    p = dict(g_mix=g_mix, g_ffn=g_ffn, g_final=g_final, na_w_qkv=na_w_qkv, na_b_qkv=na_b_qkv, na_rpb=na_rpb,
             na_w_o=na_w_o, mla_w_dq=mla_w_dq, mla_g_q=mla_g_q, mla_w_uq=mla_w_uq, mla_w_dkv=mla_w_dkv,
             mla_g_kv=mla_g_kv, mla_w_kr=mla_w_kr, mla_w_ukv=mla_w_ukv, mla_w_o=mla_w_o,
             ffn_w_gate=ffn_w_gate, ffn_w_up=ffn_w_up, ffn_w_down=ffn_w_down)
    bp, seq_len, d = x_prompt.shape
    bs = x_sample.shape[0]
    assert x_sample.shape[1:] == (seq_len, d)
    x_parts = [x_prompt.reshape(bp * seq_len, d), x_sample.reshape(bs * seq_len, d)]
    y_prompt, y_sample = trunk(x_parts, seq_len, p, TILES)
    return (y_prompt.reshape(bp, seq_len, d), y_sample.reshape(bs, seq_len, d))
```

```python
import functools

import jax
import jax.numpy as jnp
from jax import lax
from jax.experimental import pallas as pl
from jax.experimental.pallas import tpu as pltpu

GRID_W = 64
NA_HEADS = 16
NA_KR = 8
NA_KC = 16
MLA_HEADS = 16
MLA_NOPE = 128
MLA_ROPE = 64
MLA_V = 128
ROPE_THETA = 10000.0
EPS = 1e-6
LOG2_E = 1.4426950408889634

LANES = 128
BF16_SUBLANES = 16
VMEM_LIMIT_BYTES = 56 * 1024 * 1024

NA_QROWS = 4
NA_KROWS = 12
NA_KBLK_ROWS = 4
NA_UNITS = 16
NA_WINDOW_BLOCKS = NA_UNITS + NA_KROWS // NA_KBLK_ROWS - 1
MASK_VALUE = float("-inf")

MLA_QK = 256
MLA_VT_ROWS = MLA_V + BF16_SUBLANES
MLA_QCHUNK = 256

F32 = jnp.float32
BF16 = jnp.bfloat16


def _params(*sem):
    return pltpu.CompilerParams(dimension_semantics=sem, vmem_limit_bytes=VMEM_LIMIT_BYTES)


def _rmsnorm_f32(x, g):
    return x * lax.rsqrt(jnp.mean(x * x, axis=-1, keepdims=True) + EPS) * g


def _dot(a, b):
    return jnp.dot(a, b, preferred_element_type=F32)


def _dot_nt(a, b):
    return lax.dot_general(a, b, (((1,), (1,)), ((), ())), preferred_element_type=F32)


def _two_part_specs(block, n_first_tiles):
    return [pl.BlockSpec(block, lambda i, *_: (jnp.minimum(i, n_first_tiles - 1), 0)),
            pl.BlockSpec(block, lambda i, *_: (jnp.maximum(i - n_first_tiles, 0), 0))]


def _for_owning_part(x_refs, n_first_tiles, fn):
    if len(x_refs) == 1:
        fn(x_refs[0])
        return
    i = pl.program_id(0)
    pl.when(i < n_first_tiles)(lambda: fn(x_refs[0]))
    pl.when(i >= n_first_tiles)(lambda: fn(x_refs[1]))


def _input_norm_kernel(xa_ref, xb_ref, g_ref, h_ref, *, n_first_tiles):
    def write(x_ref):
        h_ref[...] = _rmsnorm_f32(x_ref[...], g_ref[...]).astype(BF16)

    _for_owning_part((xa_ref, xb_ref), n_first_tiles, write)


def input_norm(x_parts, g, *, tm):
    xa, xb = x_parts
    d = xa.shape[1]
    assert xa.shape[0] % tm == 0 and xb.shape[0] % tm == 0
    n_first_tiles = xa.shape[0] // tm
    n = xa.shape[0] + xb.shape[0]
    return pl.pallas_call(
        functools.partial(_input_norm_kernel, n_first_tiles=n_first_tiles),
        grid=(n // tm,),
        in_specs=_two_part_specs((tm, d), n_first_tiles) + [pl.BlockSpec((1, d), lambda i: (0, 0))],
        out_specs=pl.BlockSpec((tm, d), lambda i: (i, 0)),
        out_shape=jax.ShapeDtypeStruct((n, d), BF16),
        compiler_params=_params("parallel"),
        name="input_norm",
    )(xa, xb, g)


def _qkv_kernel(h_ref, w_ref, b_ref, cs_ref, o_ref):
    acc = (_dot(h_ref[...], w_ref[...]) + b_ref[...]) * cs_ref[...]
    for hh in range(o_ref.shape[0]):
        o_ref[hh] = acc[:, hh * LANES:(hh + 1) * LANES].astype(BF16)


def qkv_proj(h, w, b, cs, *, tm, tn):
    n, d = h.shape
    m = w.shape[1]
    return pl.pallas_call(
        _qkv_kernel,
        grid=(n // tm, m // tn),
        in_specs=[
            pl.BlockSpec((tm, d), lambda i, j: (i, 0)),
            pl.BlockSpec((d, tn), lambda i, j: (0, j)),
            pl.BlockSpec((1, tn), lambda i, j: (0, j)),
            pl.BlockSpec((1, tn), lambda i, j: (0, j)),
        ],
        out_specs=pl.BlockSpec((tn // LANES, tm, LANES), lambda i, j: (j, i, 0)),
        out_shape=jax.ShapeDtypeStruct((m // LANES, n, LANES), BF16),
        compiler_params=_params("parallel", "parallel"),
        name="qkv_proj",
    )(h, w, b, cs)


def _na_kernel(q_ref, k_ref, v_ref, bias_ref, o_ref, *, n_units):
    step = pl.program_id(2)
    tq = NA_QROWS * GRID_W
    tk = NA_KROWS * GRID_W
    window_base = _na_window_base(step, n_units)
    for u in range(NA_UNITS):
        unit = NA_UNITS * step + u
        start = pl.multiple_of((_na_key_base(unit, n_units) - window_base) * tq, tq)
        k = k_ref[0, pl.ds(start, tk), :]
        v = v_ref[0, pl.ds(start, tk), :]
        cls = jnp.where(unit == 0, 0, jnp.where(unit == n_units - 1, 2, 1))
        rows = pl.ds(u * tq, tq)
        s = _dot_nt(q_ref[0, rows, :], k) + bias_ref[cls, 0]
        m = jnp.max(s, axis=-1, keepdims=True)
        p = jnp.exp2(s - m)
        l = jnp.sum(p, axis=-1, keepdims=True)
        o = _dot(p.astype(BF16), v)
        o_ref[0, rows, :] = (o / l).astype(BF16)


def _na_key_base(unit, n_units):
    return jnp.clip(unit - 1, 0, n_units - NA_KROWS // NA_KBLK_ROWS)


def _na_window_base(step, n_units):
    return jnp.clip(NA_UNITS * step - 1, 0, n_units - NA_WINDOW_BLOCKS)


def na_bias_table(rpb, rows):
    heads = rpb.shape[0]
    nb = rows // NA_QROWS
    cols = jnp.arange(GRID_W)
    c_start = jnp.clip(cols - NA_KC // 2, 0, GRID_W - NA_KC)
    cvalid = (cols[None, :] >= c_start[:, None]) & (cols[None, :] < c_start[:, None] + NA_KC)
    coff = jnp.clip(cols[None, :] - cols[:, None] + (NA_KC - 1), 0, 2 * NA_KC - 2)
    a = rpb[:, :, coff]
    tables = []
    for b in (0, 1, nb - 1):
        r = b * NA_QROWS + jnp.arange(NA_QROWS)
        r_start = jnp.clip(r - NA_KR // 2, 0, rows - NA_KR)
        kb = NA_KBLK_ROWS * min(max(b - 1, 0), nb - NA_KROWS // NA_KBLK_ROWS)
        krow = kb + jnp.arange(NA_KROWS)
        rvalid = (krow[None, :] >= r_start[:, None]) & (krow[None, :] < r_start[:, None] + NA_KR)
        roff = jnp.clip(krow[None, :] - r[:, None] + (NA_KR - 1), 0, 2 * NA_KR - 2)
        t = a[:, roff]
        valid = rvalid[:, :, None, None] & cvalid[None, None]
        t = jnp.where(valid[None], t * LOG2_E, MASK_VALUE)
        t = t.transpose(0, 1, 3, 2, 4).reshape(heads, NA_QROWS * GRID_W, NA_KROWS * GRID_W)
        tables.append(t)
    return jnp.stack(tables).astype(F32)


def na_attention(qkv_hm, bias, *, n_seq, seq_len):
    heads = NA_HEADS
    rows = seq_len // GRID_W
    assert NA_QROWS == NA_KBLK_ROWS and rows % (NA_QROWS * NA_UNITS) == 0
    n_units = rows // NA_QROWS
    assert n_units >= NA_WINDOW_BLOCKS
    nb = n_units // NA_UNITS
    tq = NA_QROWS * GRID_W
    n = n_seq * seq_len

    def q_map(h, s, b):
        return (h, s * nb + b, 0)

    def kv_map(which):
        def f(h, s, b):
            return (which * heads + h, (s * n_units + _na_window_base(b, n_units)) * tq, 0)
        return f

    kv_block = (pl.Element(1), pl.Element(NA_WINDOW_BLOCKS * tq), pl.Element(LANES))
    return pl.pallas_call(
        functools.partial(_na_kernel, n_units=n_units),
        grid=(heads, n_seq, nb),
        in_specs=[
            pl.BlockSpec((1, NA_UNITS * tq, LANES), q_map),
            pl.BlockSpec(kv_block, kv_map(1)),
            pl.BlockSpec(kv_block, kv_map(2)),
            pl.BlockSpec((3, 1, tq, NA_KROWS * GRID_W), lambda h, s, b: (0, h, 0, 0)),
        ],
        out_specs=pl.BlockSpec((1, NA_UNITS * tq, LANES), q_map),
        out_shape=jax.ShapeDtypeStruct((heads, n, LANES), BF16),
        compiler_params=_params("parallel", "parallel", "arbitrary"),
        name="na_attention",
    )(qkv_hm, qkv_hm, qkv_hm, bias)


def _resident(block_shape, index_map):
    return pl.BlockSpec(block_shape, index_map, pipeline_mode=pl.Buffered(1))


def _oproj_kernel(o_ref, w_ref, *rest, n_first_tiles):
    *x_refs, g_ref, y_ref, h_ref = rest
    o = jnp.concatenate([o_ref[h] for h in range(o_ref.shape[0])], axis=-1)
    if len(x_refs) == 1:
        x = x_refs[0][...]
    else:
        x = lax.select(pl.program_id(0) < n_first_tiles, x_refs[0][...], x_refs[1][...])
    y = x + _dot(o, w_ref[...])
    y_ref[...] = y
    h_ref[...] = _rmsnorm_f32(y, g_ref[...]).astype(BF16)


def out_proj(o_hm, w, x_parts, g, *, tm):
    heads, n, dh = o_hm.shape
    d = w.shape[1]
    if len(x_parts) == 1:
        n_first_tiles, x_specs = 0, [pl.BlockSpec((tm, d), lambda i: (i, 0))]
    else:
        assert all(x.shape[0] % tm == 0 for x in x_parts)
        n_first_tiles = x_parts[0].shape[0] // tm
        x_specs = _two_part_specs((tm, d), n_first_tiles)
    return pl.pallas_call(
        functools.partial(_oproj_kernel, n_first_tiles=n_first_tiles),
        grid=(n // tm,),
        in_specs=[
            pl.BlockSpec((heads, tm, dh), lambda i: (0, i, 0)),
            _resident((heads * dh, d), lambda i: (0, 0)),
            *x_specs,
            pl.BlockSpec((1, d), lambda i: (0, 0)),
        ],
        out_specs=[pl.BlockSpec((tm, d), lambda i: (i, 0)), pl.BlockSpec((tm, d), lambda i: (i, 0))],
        out_shape=[jax.ShapeDtypeStruct((n, d), F32), jax.ShapeDtypeStruct((n, d), BF16)],
        compiler_params=_params("parallel"),
        name="out_proj",
    )(o_hm, w, *x_parts, g)


def _ffn_up_kernel(h_ref, wg_ref, wu_ref, a_ref):
    h = h_ref[...]
    gate = _dot(h, wg_ref[...])
    up = _dot(h, wu_ref[...])
    a_ref[...] = (gate * jax.nn.sigmoid(gate) * up).astype(BF16)


def ffn_up(h, wg, wu, *, tm, tn):
    n, d = h.shape
    f = wg.shape[1]
    return pl.pallas_call(
        _ffn_up_kernel,
        grid=(n // tm, f // tn),
        in_specs=[
            pl.BlockSpec((tm, d), lambda i, j: (i, 0)),
            pl.BlockSpec((d, tn), lambda i, j: (0, j)),
            pl.BlockSpec((d, tn), lambda i, j: (0, j)),
        ],
        out_specs=pl.BlockSpec((tm, tn), lambda i, j: (i, j)),
        out_shape=jax.ShapeDtypeStruct((n, f), BF16),
        compiler_params=_params("parallel", "parallel"),
        name="ffn_up",
    )(h, wg, wu)


def _ffn_down_kernel(a_ref, w_ref, x_ref, g_ref, y_ref, h_ref):
    y = x_ref[...] + _dot(a_ref[...], w_ref[...])
    y_ref[...] = y
    h_ref[...] = _rmsnorm_f32(y, g_ref[...]).astype(BF16)


def _ffn_down_final_kernel(a_ref, w_ref, x_ref, g_ref, y_ref):
    y_ref[...] = _rmsnorm_f32(x_ref[...] + _dot(a_ref[...], w_ref[...]), g_ref[...])


def ffn_down(a, w, x, g, *, tm, final=False, row_start=0, n_rows=None):
    f = a.shape[1]
    d = w.shape[1]
    n_rows = a.shape[0] if n_rows is None else n_rows
    off = row_start // tm
    assert row_start % tm == 0 and n_rows % tm == 0
    in_specs = [
        pl.BlockSpec((tm, f), lambda i: (i + off, 0)),
        _resident((f, d), lambda i: (0, 0)),
        pl.BlockSpec((tm, d), lambda i: (i + off, 0)),
        pl.BlockSpec((1, d), lambda i: (0, 0)),
    ]
    row_block = pl.BlockSpec((tm, d), lambda i: (i, 0))
    if final:
        body, out_specs, out_shape = _ffn_down_final_kernel, row_block, jax.ShapeDtypeStruct((n_rows, d), F32)
    else:
        body, out_specs = _ffn_down_kernel, [row_block, row_block]
        out_shape = [jax.ShapeDtypeStruct((n_rows, d), F32), jax.ShapeDtypeStruct((n_rows, d), BF16)]
    return pl.pallas_call(
        body,
        grid=(n_rows // tm,),
        in_specs=in_specs,
        out_specs=out_specs,
        out_shape=out_shape,
        compiler_params=_params("parallel"),
        name="ffn_down_final" if final else "ffn_down",
    )(a, w, x, g)


def _rope_pair(g128, c2):
    prod = g128 * c2
    both = prod + pltpu.roll(prod, LANES // 2, axis=1)
    lane = lax.broadcasted_iota(jnp.int32, both.shape, 1)
    return jnp.where(lane < MLA_ROPE, both, 0.0)


def _mla_down_kernel(h_ref, w_ref, gq_ref, gkv_ref, c2_ref, cq_ref, ckv_ref, kr_ref, *, q_rank, kv_rank):
    c = _dot(h_ref[...], w_ref[...])
    cq_ref[...] = _rmsnorm_f32(c[:, :q_rank], gq_ref[...]).astype(BF16)
    ckv_ref[...] = _rmsnorm_f32(c[:, q_rank:q_rank + kv_rank], gkv_ref[...]).astype(BF16)
    kr_ref[...] = _rope_pair(c[:, q_rank + kv_rank:], c2_ref[...]).astype(BF16)


def mla_down(h, w, gq, gkv, c2, *, tm, seq_len):
    n, d = h.shape
    q_rank, kv_rank = gq.shape[1], gkv.shape[1]
    width = w.shape[1]
    assert width == q_rank + kv_rank + LANES
    pos_blocks = seq_len // tm
    return pl.pallas_call(
        functools.partial(_mla_down_kernel, q_rank=q_rank, kv_rank=kv_rank),
        grid=(n // tm,),
        in_specs=[
            pl.BlockSpec((tm, d), lambda i: (i, 0)),
            _resident((d, width), lambda i: (0, 0)),
            pl.BlockSpec((1, q_rank), lambda i: (0, 0)),
            pl.BlockSpec((1, kv_rank), lambda i: (0, 0)),
            pl.BlockSpec((tm, LANES), lambda i: (i % pos_blocks, 0)),
        ],
        out_specs=[
            pl.BlockSpec((tm, q_rank), lambda i: (i, 0)),
            pl.BlockSpec((tm, kv_rank), lambda i: (i, 0)),
            pl.BlockSpec((tm, LANES), lambda i: (i, 0)),
        ],
        out_shape=[
            jax.ShapeDtypeStruct((n, q_rank), BF16),
            jax.ShapeDtypeStruct((n, kv_rank), BF16),
            jax.ShapeDtypeStruct((n, LANES), BF16),
        ],
        compiler_params=_params("parallel"),
        name="mla_down",
    )(h, w, gq, gkv, c2)


def _mla_q_kernel(cq_ref, w_ref, c2_ref, q_ref, *, scale):
    acc = _dot(cq_ref[...], w_ref[...])
    c2 = c2_ref[...]
    for hh in range(q_ref.shape[0]):
        base = hh * MLA_QK
        q_ref[hh, :, :MLA_NOPE] = (acc[:, base:base + MLA_NOPE] * scale).astype(BF16)
        rope = _rope_pair(acc[:, base + MLA_NOPE:base + MLA_QK], c2)
        q_ref[hh, :, MLA_NOPE:] = (rope * scale).astype(BF16)


def mla_q_proj(cq, w, c2, *, tm, heads_per_step, seq_len, scale):
    n, q_rank = cq.shape
    heads = w.shape[1] // MLA_QK
    tn = heads_per_step * MLA_QK
    pos_blocks = seq_len // tm
    return pl.pallas_call(
        functools.partial(_mla_q_kernel, scale=scale),
        grid=(n // tm, heads // heads_per_step),
        in_specs=[
            pl.BlockSpec((tm, q_rank), lambda i, j: (i, 0)),
            pl.BlockSpec((q_rank, tn), lambda i, j: (0, j)),
            pl.BlockSpec((tm, LANES), lambda i, j: (i % pos_blocks, 0)),
        ],
        out_specs=pl.BlockSpec((heads_per_step, tm, MLA_QK), lambda i, j: (j, i, 0)),
        out_shape=jax.ShapeDtypeStruct((heads, n, MLA_QK), BF16),
        compiler_params=_params("parallel", "parallel"),
        name="mla_q_proj",
    )(cq, w, c2)


def _mla_kv_kernel(ckv_ref, kr_ref, wk_ref, wvt_ref, k_ref, vt_ref):
    ckv = ckv_ref[...]
    kn = _dot(ckv, wk_ref[...])
    vt = _dot_nt(wvt_ref[...], ckv)
    kr = kr_ref[...]
    tm = ckv.shape[0]
    row = lax.broadcasted_iota(jnp.int32, (BF16_SUBLANES, tm), 0)
    ones_rows = jnp.where(row == 0, 1.0, 0.0).astype(BF16)
    for hh in range(k_ref.shape[0]):
        k_ref[hh, :, :MLA_NOPE] = kn[:, hh * MLA_NOPE:(hh + 1) * MLA_NOPE].astype(BF16)
        k_ref[hh, :, MLA_NOPE:] = kr
        vt_ref[hh, :MLA_V, :] = vt[hh * MLA_V:(hh + 1) * MLA_V, :].astype(BF16)
        vt_ref[hh, MLA_V:, :] = ones_rows


def mla_kv_proj(ckv, kr, wk, wvt, *, tm, heads_per_step):
    n, kv_rank = ckv.shape
    heads = wk.shape[1] // MLA_NOPE
    hps = heads_per_step
    return pl.pallas_call(
        _mla_kv_kernel,
        grid=(n // tm, heads // hps),
        in_specs=[
            pl.BlockSpec((tm, kv_rank), lambda i, j: (i, 0)),
            pl.BlockSpec((tm, LANES), lambda i, j: (i, 0)),
            pl.BlockSpec((kv_rank, hps * MLA_NOPE), lambda i, j: (0, j)),
            pl.BlockSpec((hps * MLA_V, kv_rank), lambda i, j: (j, 0)),
        ],
        out_specs=[
            pl.BlockSpec((hps, tm, MLA_QK), lambda i, j: (j, i, 0)),
            pl.BlockSpec((hps, MLA_VT_ROWS, tm), lambda i, j: (j, 0, i)),
        ],
        out_shape=[
            jax.ShapeDtypeStruct((heads, n, MLA_QK), BF16),
            jax.ShapeDtypeStruct((heads, MLA_VT_ROWS, n), BF16),
        ],
        compiler_params=_params("parallel", "parallel"),
        name="mla_kv_proj",
    )(ckv, kr, wk, wvt)


def _mla_attn_kernel(q_ref, k_ref, vt_ref, o_ref, m_ref, acc_ref, s0_ref, s1_ref, mt0_ref, mt1_ref, *, tk, unroll):
    n_kt = k_ref.shape[1] // tk
    assert unroll % 2 == 0
    s_refs, mt_refs = (s0_ref, s1_ref), (mt0_ref, mt1_ref)
    acc_ref[...] = jnp.zeros(acc_ref.shape, F32)

    tq = q_ref.shape[1]
    chunks = [pl.ds(c * MLA_QCHUNK, MLA_QCHUNK) for c in range(tq // MLA_QCHUNK)]

    def scores(kt, slot, cols):
        start = pl.multiple_of(kt * tk, tk)
        s = _dot_nt(k_ref[0, pl.ds(start, tk), :], q_ref[0, cols, :])
        s_refs[slot][:, cols] = s
        mt_refs[slot][:, cols] = jnp.max(s, axis=0, keepdims=True)

    def probs(slot, cols):
        m_prev = m_ref[:, cols]
        m_new = jnp.maximum(m_prev, mt_refs[slot][:, cols])
        m_ref[:, cols] = m_new
        return jnp.exp2(s_refs[slot][:, cols] - m_new).astype(BF16), jnp.exp2(m_prev - m_new)

    def accumulate(kt, cols, p, alpha):
        start = pl.multiple_of(kt * tk, tk)
        acc_ref[:, cols] = acc_ref[:, cols] * alpha + _dot(vt_ref[0, :, pl.ds(start, tk)], p)

    def step(kt, slot, with_next):
        for cols in chunks:
            p, alpha = probs(slot, cols)
            if with_next:
                scores(kt + 1, 1 - slot, cols)
            accumulate(kt, cols, p, alpha)

    for cols in chunks:
        scores(0, 0, cols)
    m_ref[...] = mt0_ref[...]
    n_loop = (n_kt - 1) // unroll

    def body(j, carry):
        for u in range(unroll):
            step(unroll * j + u, u % 2, True)
        return carry

    lax.fori_loop(0, n_loop, body, 0)
    for kt in range(n_loop * unroll, n_kt):
        step(kt, kt % 2, kt + 1 < n_kt)
    acc = acc_ref[...]
    o_t = acc[:MLA_V, :] / acc[MLA_V:MLA_V + 1, :]
    o_ref[0] = o_t.T.astype(BF16)


def mla_attention(q_hm, k_hm, vt_hm, *, n_seq, seq_len, tq, tk, unroll):
    heads, n, _ = q_hm.shape
    nq = seq_len // tq
    return pl.pallas_call(
        functools.partial(_mla_attn_kernel, tk=tk, unroll=unroll),
        grid=(n_seq, heads, nq),
        in_specs=[
            pl.BlockSpec((1, tq, MLA_QK), lambda s, h, i: (h, s * nq + i, 0)),
            pl.BlockSpec((1, seq_len, MLA_QK), lambda s, h, i: (h, s, 0)),
            pl.BlockSpec((1, MLA_VT_ROWS, seq_len), lambda s, h, i: (h, 0, s)),
        ],
        out_specs=pl.BlockSpec((1, tq, MLA_V), lambda s, h, i: (h, s * nq + i, 0)),
        out_shape=jax.ShapeDtypeStruct((heads, n, MLA_V), BF16),
        scratch_shapes=[pltpu.VMEM((1, tq), F32), pltpu.VMEM((MLA_VT_ROWS, tq), F32),
                        pltpu.VMEM((tk, tq), F32), pltpu.VMEM((tk, tq), F32),
                        pltpu.VMEM((1, tq), F32), pltpu.VMEM((1, tq), F32)],
        compiler_params=_params("parallel", "parallel", "arbitrary"),
        name="mla_attention",
    )(q_hm, k_hm, vt_hm)


def rope_table(seq_len):
    inv_freq = ROPE_THETA ** (-jnp.arange(0, MLA_ROPE, 2, dtype=F32) / MLA_ROPE)
    ang = jnp.arange(seq_len, dtype=F32)[:, None] * inv_freq[None, :]
    cos, sin = jnp.cos(ang), jnp.sin(ang)
    return jnp.concatenate([cos, cos, -sin, sin], axis=-1)


def _swap_halves(w):
    half = w.shape[-1] // 2
    return jnp.concatenate([w[..., half:], w[..., :half]], axis=-1)


def trunk(x_parts, seq_len, p, tiles):
    d = x_parts[0].shape[1]
    n = sum(x.shape[0] for x in x_parts)
    n_seq = n // seq_len
    rows = seq_len // GRID_W
    row2 = lambda v: v.reshape(1, -1)

    w_qkv = p["na_w_qkv"][0].astype(BF16)
    na_dh = d // NA_HEADS
    col_scale = jnp.concatenate([jnp.full((d,), LOG2_E * na_dh ** -0.5, F32), jnp.ones((2 * d,), F32)])
    h = input_norm(x_parts, row2(p["g_mix"][0]), tm=tiles["tm"])
    qkv = qkv_proj(h, w_qkv, row2(p["na_b_qkv"][0]), row2(col_scale), tm=tiles["qkv_tm"], tn=tiles["qkv_tn"])
    bias = na_bias_table(p["na_rpb"][0], rows)
    o = na_attention(qkv, bias, n_seq=n_seq, seq_len=seq_len)
    x, h = out_proj(o, p["na_w_o"][0].astype(BF16), x_parts, row2(p["g_ffn"][0]), tm=tiles["tm"])
    a = _ffn_up(h, p, 0, tiles)
    x, h = ffn_down(a, p["ffn_w_down"][0].astype(BF16), x, row2(p["g_mix"][1]), tm=tiles["down_tm"])

    w_kr = p["mla_w_kr"][0]
    w_down = jnp.concatenate([p["mla_w_dq"][0], p["mla_w_dkv"][0], w_kr, _swap_halves(w_kr)], axis=1).astype(BF16)
    c2 = rope_table(seq_len)
    cq, ckv, kr = mla_down(h, w_down, row2(p["mla_g_q"][0]), row2(p["mla_g_kv"][0]), c2,
                           tm=tiles["tm"], seq_len=seq_len)
    q_rank = cq.shape[1]
    w_uq = p["mla_w_uq"][0].reshape(q_rank, MLA_HEADS, MLA_NOPE + MLA_ROPE)
    w_uq = jnp.concatenate([w_uq, _swap_halves(w_uq[..., MLA_NOPE:])], axis=-1)
    w_uq = w_uq.reshape(q_rank, MLA_HEADS * MLA_QK).astype(BF16)
    q_hm = mla_q_proj(cq, w_uq, c2, tm=tiles["tm"], heads_per_step=tiles["q_hps"], seq_len=seq_len,
                      scale=LOG2_E * (MLA_NOPE + MLA_ROPE) ** -0.5)
    kv_rank = ckv.shape[1]
    w_ukv = p["mla_w_ukv"][0].reshape(kv_rank, MLA_HEADS, MLA_NOPE + MLA_V)
    w_uk = w_ukv[..., :MLA_NOPE].reshape(kv_rank, MLA_HEADS * MLA_NOPE).astype(BF16)
    w_uvt = w_ukv[..., MLA_NOPE:].reshape(kv_rank, MLA_HEADS * MLA_V).T.astype(BF16)
    k_hm, vt_hm = mla_kv_proj(ckv, kr, w_uk, w_uvt, tm=tiles["tm"], heads_per_step=tiles["kv_hps"])
    o = mla_attention(q_hm, k_hm, vt_hm, n_seq=n_seq, seq_len=seq_len, tq=tiles["mla_tq"], tk=tiles["mla_tk"],
                      unroll=tiles["mla_unroll"])
    x, h = out_proj(o, p["mla_w_o"][0].astype(BF16), [x], row2(p["g_ffn"][1]), tm=tiles["tm"])
    a = _ffn_up(h, p, 1, tiles)
    w_down = p["ffn_w_down"][1].astype(BF16)
    outs, row_start = [], 0
    for part in x_parts:
        outs.append(ffn_down(a, w_down, x, row2(p["g_final"]), tm=tiles["down_tm"], final=True,
                             row_start=row_start, n_rows=part.shape[0]))
        row_start += part.shape[0]
    return outs


def _ffn_up(h, p, layer, tiles):
    return ffn_up(h, p["ffn_w_gate"][layer].astype(BF16), p["ffn_w_up"][layer].astype(BF16),
                  tm=tiles["ffn_tm"], tn=tiles["ffn_tn"])


TILES = dict(tm=512, qkv_tm=1024, qkv_tn=1024, ffn_tm=1024, ffn_tn=512, down_tm=256, q_hps=16, kv_hps=16,
             mla_tq=2048, mla_tk=1024, mla_unroll=4)


def kernel(x_prompt, x_sample, g_mix, g_ffn, g_final, na_w_qkv, na_b_qkv, na_rpb, na_w_o, mla_w_dq, mla_g_q,
           mla_w_uq, mla_w_dkv, mla_g_kv, mla_w_kr, mla_w_ukv, mla_w_o, ffn_w_gate, ffn_w_up, ffn_w_down):
    p = dict(g_mix=g_mix, g_ffn=g_ffn, g_final=g_final, na_w_qkv=na_w_qkv, na_b_qkv=na_b_qkv, na_rpb=na_rpb,
             na_w_o=na_w_o, mla_w_dq=mla_w_dq, mla_g_q=mla_g_q, mla_w_uq=mla_w_uq, mla_w_dkv=mla_w_dkv,
             mla_g_kv=mla_g_kv, mla_w_kr=mla_w_kr, mla_w_ukv=mla_w_ukv, mla_w_o=mla_w_o,
             ffn_w_gate=ffn_w_gate, ffn_w_up=ffn_w_up, ffn_w_down=ffn_w_down)
    bp, seq_len, d = x_prompt.shape
    bs = x_sample.shape[0]
    assert x_sample.shape[1:] == (seq_len, d)
    x_parts = [x_prompt.reshape(bp * seq_len, d), x_sample.reshape(bs * seq_len, d)]
    y_prompt, y_sample = trunk(x_parts, seq_len, p, TILES)
    return (y_prompt.reshape(bp, seq_len, d), y_sample.reshape(bs, seq_len, d))
```

```python
import functools

import jax
import jax.numpy as jnp
from jax import lax
from jax.experimental import pallas as pl
from jax.experimental.pallas import tpu as pltpu

GRID_W = 64
NA_HEADS = 16
NA_KR = 8
NA_KC = 16
MLA_HEADS = 16
MLA_NOPE = 128
MLA_ROPE = 64
MLA_V = 128
ROPE_THETA = 10000.0
EPS = 1e-6
LOG2_E = 1.4426950408889634

LANES = 128
BF16_SUBLANES = 16
VMEM_LIMIT_BYTES = 56 * 1024 * 1024

NA_QROWS = 4
NA_KROWS = 12
NA_KBLK_ROWS = 4
NA_UNITS = 16
NA_WINDOW_BLOCKS = NA_UNITS + NA_KROWS // NA_KBLK_ROWS - 1
MASK_VALUE = float("-inf")

MLA_QK = 256
MLA_VT_ROWS = MLA_V + BF16_SUBLANES
MLA_QCHUNK = 256

F32 = jnp.float32
BF16 = jnp.bfloat16


def _params(*sem):
    return pltpu.CompilerParams(dimension_semantics=sem, vmem_limit_bytes=VMEM_LIMIT_BYTES)


def _rmsnorm_f32(x, g):
    return x * lax.rsqrt(jnp.mean(x * x, axis=-1, keepdims=True) + EPS) * g


def _dot(a, b):
    return jnp.dot(a, b, preferred_element_type=F32)


def _dot_nt(a, b):
    return lax.dot_general(a, b, (((1,), (1,)), ((), ())), preferred_element_type=F32)


def _two_part_specs(block, n_first_tiles):
    return [pl.BlockSpec(block, lambda i, *_: (jnp.minimum(i, n_first_tiles - 1), 0)),
            pl.BlockSpec(block, lambda i, *_: (jnp.maximum(i - n_first_tiles, 0), 0))]


def _for_owning_part(x_refs, n_first_tiles, fn):
    i = pl.program_id(0)
    pl.when(i < n_first_tiles)(lambda: fn(x_refs[0]))
    pl.when(i >= n_first_tiles)(lambda: fn(x_refs[1]))


def _input_norm_kernel(xa_ref, xb_ref, g_ref, h_ref, *, n_first_tiles):
    def write(x_ref):
        h_ref[...] = _rmsnorm_f32(x_ref[...], g_ref[...]).astype(BF16)

    _for_owning_part((xa_ref, xb_ref), n_first_tiles, write)


def input_norm(x_parts, g, *, tm):
    xa, xb = x_parts
    d = xa.shape[1]
    assert xa.shape[0] % tm == 0 and xb.shape[0] % tm == 0
    n_first_tiles = xa.shape[0] // tm
    n = xa.shape[0] + xb.shape[0]
    return pl.pallas_call(
        functools.partial(_input_norm_kernel, n_first_tiles=n_first_tiles),
        grid=(n // tm,),
        in_specs=_two_part_specs((tm, d), n_first_tiles) + [pl.BlockSpec((1, d), lambda i: (0, 0))],
        out_specs=pl.BlockSpec((tm, d), lambda i: (i, 0)),
        out_shape=jax.ShapeDtypeStruct((n, d), BF16),
        compiler_params=_params("parallel"),
        name="input_norm",
    )(xa, xb, g)


def _qkv_kernel(h_ref, w_ref, b_ref, cs_ref, o_ref):
    acc = (_dot(h_ref[...], w_ref[...]) + b_ref[...]) * cs_ref[...]
    for hh in range(o_ref.shape[0]):
        o_ref[hh] = acc[:, hh * LANES:(hh + 1) * LANES].astype(BF16)


def qkv_proj(h, w, b, cs, *, tm, tn):
    n, d = h.shape
    m = w.shape[1]
    return pl.pallas_call(
        _qkv_kernel,
        grid=(n // tm, m // tn),
        in_specs=[
            pl.BlockSpec((tm, d), lambda i, j: (i, 0)),
            pl.BlockSpec((d, tn), lambda i, j: (0, j)),
            pl.BlockSpec((1, tn), lambda i, j: (0, j)),
            pl.BlockSpec((1, tn), lambda i, j: (0, j)),
        ],
        out_specs=pl.BlockSpec((tn // LANES, tm, LANES), lambda i, j: (j, i, 0)),
        out_shape=jax.ShapeDtypeStruct((m // LANES, n, LANES), BF16),
        compiler_params=_params("parallel", "parallel"),
        name="qkv_proj",
    )(h, w, b, cs)


def _na_kernel(q_ref, k_ref, v_ref, bias_ref, o_ref, *, n_units):
    step = pl.program_id(2)
    tq = NA_QROWS * GRID_W
    tk = NA_KROWS * GRID_W
    window_base = _na_window_base(step, n_units)
    for u in range(NA_UNITS):
        unit = NA_UNITS * step + u
        start = pl.multiple_of((_na_key_base(unit, n_units) - window_base) * tq, tq)
        k = k_ref[0, pl.ds(start, tk), :]
        v = v_ref[0, pl.ds(start, tk), :]
        cls = jnp.where(unit == 0, 0, jnp.where(unit == n_units - 1, 2, 1))
        rows = pl.ds(u * tq, tq)
        s = _dot_nt(q_ref[0, rows, :], k) + bias_ref[cls, 0]
        m = jnp.max(s, axis=-1, keepdims=True)
        p = jnp.exp2(s - m)
        l = jnp.sum(p, axis=-1, keepdims=True)
        o = _dot(p.astype(BF16), v)
        o_ref[0, rows, :] = (o / l).astype(BF16)


def _na_key_base(unit, n_units):
    return jnp.clip(unit - 1, 0, n_units - NA_KROWS // NA_KBLK_ROWS)


def _na_window_base(step, n_units):
    return jnp.clip(NA_UNITS * step - 1, 0, n_units - NA_WINDOW_BLOCKS)


def na_bias_table(rpb, rows):
    heads = rpb.shape[0]
    nb = rows // NA_QROWS
    cols = jnp.arange(GRID_W)
    c_start = jnp.clip(cols - NA_KC // 2, 0, GRID_W - NA_KC)
    cvalid = (cols[None, :] >= c_start[:, None]) & (cols[None, :] < c_start[:, None] + NA_KC)
    coff = jnp.clip(cols[None, :] - cols[:, None] + (NA_KC - 1), 0, 2 * NA_KC - 2)
    a = rpb[:, :, coff]
    tables = []
    for b in (0, 1, nb - 1):
        r = b * NA_QROWS + jnp.arange(NA_QROWS)
        r_start = jnp.clip(r - NA_KR // 2, 0, rows - NA_KR)
        kb = NA_KBLK_ROWS * min(max(b - 1, 0), nb - NA_KROWS // NA_KBLK_ROWS)
        krow = kb + jnp.arange(NA_KROWS)
        rvalid = (krow[None, :] >= r_start[:, None]) & (krow[None, :] < r_start[:, None] + NA_KR)
        roff = jnp.clip(krow[None, :] - r[:, None] + (NA_KR - 1), 0, 2 * NA_KR - 2)
        t = a[:, roff]
        valid = rvalid[:, :, None, None] & cvalid[None, None]
        t = jnp.where(valid[None], t * LOG2_E, MASK_VALUE)
        t = t.transpose(0, 1, 3, 2, 4).reshape(heads, NA_QROWS * GRID_W, NA_KROWS * GRID_W)
        tables.append(t)
    return jnp.stack(tables).astype(F32)


def na_attention(qkv_hm, bias, *, n_seq, seq_len):
    heads = NA_HEADS
    rows = seq_len // GRID_W
    assert NA_QROWS == NA_KBLK_ROWS and rows % (NA_QROWS * NA_UNITS) == 0
    n_units = rows // NA_QROWS
    assert n_units >= NA_WINDOW_BLOCKS
    nb = n_units // NA_UNITS
    tq = NA_QROWS * GRID_W
    n = n_seq * seq_len

    def q_map(h, s, b):
        return (h, s * nb + b, 0)

    def kv_map(which):
        def f(h, s, b):
            return (which * heads + h, (s * n_units + _na_window_base(b, n_units)) * tq, 0)
        return f

    kv_block = (pl.Element(1), pl.Element(NA_WINDOW_BLOCKS * tq), pl.Element(LANES))
    return pl.pallas_call(
        functools.partial(_na_kernel, n_units=n_units),
        grid=(heads, n_seq, nb),
        in_specs=[
            pl.BlockSpec((1, NA_UNITS * tq, LANES), q_map),
            pl.BlockSpec(kv_block, kv_map(1)),
            pl.BlockSpec(kv_block, kv_map(2)),
            pl.BlockSpec((3, 1, tq, NA_KROWS * GRID_W), lambda h, s, b: (0, h, 0, 0)),
        ],
        out_specs=pl.BlockSpec((1, NA_UNITS * tq, LANES), q_map),
        out_shape=jax.ShapeDtypeStruct((heads, n, LANES), BF16),
        compiler_params=_params("parallel", "parallel", "arbitrary"),
        name="na_attention",
    )(qkv_hm, qkv_hm, qkv_hm, bias)


def _resident(block_shape, index_map):
    return pl.BlockSpec(block_shape, index_map, pipeline_mode=pl.Buffered(1))


def _oproj_kernel(o_ref, w_ref, *rest, n_first_tiles):
    *x_refs, g_ref, y_ref, h_ref = rest
    o = jnp.concatenate([o_ref[h] for h in range(o_ref.shape[0])], axis=-1)
    if len(x_refs) == 1:
        x = x_refs[0][...]
    else:
        x = lax.select(pl.program_id(0) < n_first_tiles, x_refs[0][...], x_refs[1][...])
    y = x + _dot(o, w_ref[...])
    y_ref[...] = y
    h_ref[...] = _rmsnorm_f32(y, g_ref[...]).astype(BF16)


def out_proj(o_hm, w, x_parts, g, *, tm):
    heads, n, dh = o_hm.shape
    d = w.shape[1]
    if len(x_parts) == 1:
        n_first_tiles, x_specs = 0, [pl.BlockSpec((tm, d), lambda i: (i, 0))]
    else:
        assert all(x.shape[0] % tm == 0 for x in x_parts)
        n_first_tiles = x_parts[0].shape[0] // tm
        x_specs = _two_part_specs((tm, d), n_first_tiles)
    return pl.pallas_call(
        functools.partial(_oproj_kernel, n_first_tiles=n_first_tiles),
        grid=(n // tm,),
        in_specs=[
            pl.BlockSpec((heads, tm, dh), lambda i: (0, i, 0)),
            _resident((heads * dh, d), lambda i: (0, 0)),
            *x_specs,
            pl.BlockSpec((1, d), lambda i: (0, 0)),
        ],
        out_specs=[pl.BlockSpec((tm, d), lambda i: (i, 0)), pl.BlockSpec((tm, d), lambda i: (i, 0))],
        out_shape=[jax.ShapeDtypeStruct((n, d), F32), jax.ShapeDtypeStruct((n, d), BF16)],
        compiler_params=_params("parallel"),
        name="out_proj",
    )(o_hm, w, *x_parts, g)


def _ffn_up_kernel(h_ref, wg_ref, wu_ref, a_ref):
    h = h_ref[...]
    gate = _dot(h, wg_ref[...])
    up = _dot(h, wu_ref[...])
    a_ref[...] = (gate * jax.nn.sigmoid(gate) * up).astype(BF16)


def ffn_up(h, wg, wu, *, tm, tn):
    n, d = h.shape
    f = wg.shape[1]
    return pl.pallas_call(
        _ffn_up_kernel,
        grid=(n // tm, f // tn),
        in_specs=[
            pl.BlockSpec((tm, d), lambda i, j: (i, 0)),
            pl.BlockSpec((d, tn), lambda i, j: (0, j)),
            pl.BlockSpec((d, tn), lambda i, j: (0, j)),
        ],
        out_specs=pl.BlockSpec((tm, tn), lambda i, j: (i, j)),
        out_shape=jax.ShapeDtypeStruct((n, f), BF16),
        compiler_params=_params("parallel", "parallel"),
        name="ffn_up",
    )(h, wg, wu)


def _ffn_down_kernel(a_ref, w_ref, x_ref, g_ref, y_ref, h_ref):
    y = x_ref[...] + _dot(a_ref[...], w_ref[...])
    y_ref[...] = y
    h_ref[...] = _rmsnorm_f32(y, g_ref[...]).astype(BF16)


def _ffn_down_final_kernel(a_ref, w_ref, x_ref, g_ref, y_ref):
    y_ref[...] = _rmsnorm_f32(x_ref[...] + _dot(a_ref[...], w_ref[...]), g_ref[...])


def ffn_down(a, w, x, g, *, tm, final=False, row_start=0, n_rows=None):
    f = a.shape[1]
    d = w.shape[1]
    n_rows = a.shape[0] if n_rows is None else n_rows
    off = row_start // tm
    assert row_start % tm == 0 and n_rows % tm == 0
    in_specs = [
        pl.BlockSpec((tm, f), lambda i: (i + off, 0)),
        _resident((f, d), lambda i: (0, 0)),
        pl.BlockSpec((tm, d), lambda i: (i + off, 0)),
        pl.BlockSpec((1, d), lambda i: (0, 0)),
    ]
    row_block = pl.BlockSpec((tm, d), lambda i: (i, 0))
    if final:
        body, out_specs, out_shape = _ffn_down_final_kernel, row_block, jax.ShapeDtypeStruct((n_rows, d), F32)
    else:
        body, out_specs = _ffn_down_kernel, [row_block, row_block]
        out_shape = [jax.ShapeDtypeStruct((n_rows, d), F32), jax.ShapeDtypeStruct((n_rows, d), BF16)]
    return pl.pallas_call(
        body,
        grid=(n_rows // tm,),
        in_specs=in_specs,
        out_specs=out_specs,
        out_shape=out_shape,
        compiler_params=_params("parallel"),
        name="ffn_down_final" if final else "ffn_down",
    )(a, w, x, g)


def _rope_pair(g128, c2):
    prod = g128 * c2
    both = prod + pltpu.roll(prod, LANES // 2, axis=1)
    lane = lax.broadcasted_iota(jnp.int32, both.shape, 1)
    return jnp.where(lane < MLA_ROPE, both, 0.0)


def _mla_down_kernel(h_ref, w_ref, gq_ref, gkv_ref, c2_ref, cq_ref, ckv_ref, kr_ref, *, q_rank, kv_rank):
    c = _dot(h_ref[...], w_ref[...])
    cq_ref[...] = _rmsnorm_f32(c[:, :q_rank], gq_ref[...]).astype(BF16)
    ckv_ref[...] = _rmsnorm_f32(c[:, q_rank:q_rank + kv_rank], gkv_ref[...]).astype(BF16)
    kr_ref[...] = _rope_pair(c[:, q_rank + kv_rank:], c2_ref[...]).astype(BF16)


def mla_down(h, w, gq, gkv, c2, *, tm, seq_len):
    n, d = h.shape
    q_rank, kv_rank = gq.shape[1], gkv.shape[1]
    width = w.shape[1]
    assert width == q_rank + kv_rank + LANES
    pos_blocks = seq_len // tm
    return pl.pallas_call(
        functools.partial(_mla_down_kernel, q_rank=q_rank, kv_rank=kv_rank),
        grid=(n // tm,),
        in_specs=[
            pl.BlockSpec((tm, d), lambda i: (i, 0)),
            _resident((d, width), lambda i: (0, 0)),
            pl.BlockSpec((1, q_rank), lambda i: (0, 0)),
            pl.BlockSpec((1, kv_rank), lambda i: (0, 0)),
            pl.BlockSpec((tm, LANES), lambda i: (i % pos_blocks, 0)),
        ],
        out_specs=[
            pl.BlockSpec((tm, q_rank), lambda i: (i, 0)),
            pl.BlockSpec((tm, kv_rank), lambda i: (i, 0)),
            pl.BlockSpec((tm, LANES), lambda i: (i, 0)),
        ],
        out_shape=[
            jax.ShapeDtypeStruct((n, q_rank), BF16),
            jax.ShapeDtypeStruct((n, kv_rank), BF16),
            jax.ShapeDtypeStruct((n, LANES), BF16),
        ],
        compiler_params=_params("parallel"),
        name="mla_down",
    )(h, w, gq, gkv, c2)


def _mla_q_kernel(cq_ref, w_ref, c2_ref, q_ref, *, scale):
    acc = _dot(cq_ref[...], w_ref[...])
    c2 = c2_ref[...]
    for hh in range(q_ref.shape[0]):
        base = hh * MLA_QK
        q_ref[hh, :, :MLA_NOPE] = (acc[:, base:base + MLA_NOPE] * scale).astype(BF16)
        rope = _rope_pair(acc[:, base + MLA_NOPE:base + MLA_QK], c2)
        q_ref[hh, :, MLA_NOPE:] = (rope * scale).astype(BF16)


def mla_q_proj(cq, w, c2, *, tm, heads_per_step, seq_len, scale):
    n, q_rank = cq.shape
    heads = w.shape[1] // MLA_QK
    tn = heads_per_step * MLA_QK
    pos_blocks = seq_len // tm
    return pl.pallas_call(
        functools.partial(_mla_q_kernel, scale=scale),
        grid=(n // tm, heads // heads_per_step),
        in_specs=[
            pl.BlockSpec((tm, q_rank), lambda i, j: (i, 0)),
            pl.BlockSpec((q_rank, tn), lambda i, j: (0, j)),
            pl.BlockSpec((tm, LANES), lambda i, j: (i % pos_blocks, 0)),
        ],
        out_specs=pl.BlockSpec((heads_per_step, tm, MLA_QK), lambda i, j: (j, i, 0)),
        out_shape=jax.ShapeDtypeStruct((heads, n, MLA_QK), BF16),
        compiler_params=_params("parallel", "parallel"),
        name="mla_q_proj",
    )(cq, w, c2)


def _mla_kv_kernel(ckv_ref, kr_ref, wk_ref, wvt_ref, k_ref, vt_ref):
    ckv = ckv_ref[...]
    kn = _dot(ckv, wk_ref[...])
    vt = _dot_nt(wvt_ref[...], ckv)
    kr = kr_ref[...]
    tm = ckv.shape[0]
    row = lax.broadcasted_iota(jnp.int32, (BF16_SUBLANES, tm), 0)
    ones_rows = jnp.where(row == 0, 1.0, 0.0).astype(BF16)
    for hh in range(k_ref.shape[0]):
        k_ref[hh, :, :MLA_NOPE] = kn[:, hh * MLA_NOPE:(hh + 1) * MLA_NOPE].astype(BF16)
        k_ref[hh, :, MLA_NOPE:] = kr
        vt_ref[hh, :MLA_V, :] = vt[hh * MLA_V:(hh + 1) * MLA_V, :].astype(BF16)
        vt_ref[hh, MLA_V:, :] = ones_rows


def mla_kv_proj(ckv, kr, wk, wvt, *, tm, heads_per_step):
    n, kv_rank = ckv.shape
    heads = wk.shape[1] // MLA_NOPE
    hps = heads_per_step
    return pl.pallas_call(
        _mla_kv_kernel,
        grid=(n // tm, heads // hps),
        in_specs=[
            pl.BlockSpec((tm, kv_rank), lambda i, j: (i, 0)),
            pl.BlockSpec((tm, LANES), lambda i, j: (i, 0)),
            pl.BlockSpec((kv_rank, hps * MLA_NOPE), lambda i, j: (0, j)),
            pl.BlockSpec((hps * MLA_V, kv_rank), lambda i, j: (j, 0)),
        ],
        out_specs=[
            pl.BlockSpec((hps, tm, MLA_QK), lambda i, j: (j, i, 0)),
            pl.BlockSpec((hps, MLA_VT_ROWS, tm), lambda i, j: (j, 0, i)),
        ],
        out_shape=[
            jax.ShapeDtypeStruct((heads, n, MLA_QK), BF16),
            jax.ShapeDtypeStruct((heads, MLA_VT_ROWS, n), BF16),
        ],
        compiler_params=_params("parallel", "parallel"),
        name="mla_kv_proj",
    )(ckv, kr, wk, wvt)


def _mla_attn_kernel(q_ref, k_ref, vt_ref, o_ref, m_ref, acc_ref, s0_ref, s1_ref, mt0_ref, mt1_ref, *, tk, unroll):
    n_kt = k_ref.shape[1] // tk
    assert unroll % 2 == 0
    s_refs, mt_refs = (s0_ref, s1_ref), (mt0_ref, mt1_ref)
    acc_ref[...] = jnp.zeros(acc_ref.shape, F32)

    tq = q_ref.shape[1]
    chunks = [pl.ds(c * MLA_QCHUNK, MLA_QCHUNK) for c in range(tq // MLA_QCHUNK)]

    def scores(kt, slot, cols):
        start = pl.multiple_of(kt * tk, tk)
        s = _dot_nt(k_ref[0, pl.ds(start, tk), :], q_ref[0, cols, :])
        s_refs[slot][:, cols] = s
        mt_refs[slot][:, cols] = jnp.max(s, axis=0, keepdims=True)

    def probs(slot, cols):
        m_prev = m_ref[:, cols]
        m_new = jnp.maximum(m_prev, mt_refs[slot][:, cols])
        m_ref[:, cols] = m_new
        return jnp.exp2(s_refs[slot][:, cols] - m_new).astype(BF16), jnp.exp2(m_prev - m_new)

    def accumulate(kt, cols, p, alpha):
        start = pl.multiple_of(kt * tk, tk)
        acc_ref[:, cols] = acc_ref[:, cols] * alpha + _dot(vt_ref[0, :, pl.ds(start, tk)], p)

    def step(kt, slot, with_next):
        for cols in chunks:
            p, alpha = probs(slot, cols)
            if with_next:
                scores(kt + 1, 1 - slot, cols)
            accumulate(kt, cols, p, alpha)

    for cols in chunks:
        scores(0, 0, cols)
    m_ref[...] = mt0_ref[...]
    n_loop = (n_kt - 1) // unroll

    def body(j, carry):
        for u in range(unroll):
            step(unroll * j + u, u % 2, True)
        return carry

    lax.fori_loop(0, n_loop, body, 0)
    for kt in range(n_loop * unroll, n_kt):
        step(kt, kt % 2, kt + 1 < n_kt)
    acc = acc_ref[...]
    o_t = acc[:MLA_V, :] / acc[MLA_V:MLA_V + 1, :]
    o_ref[0] = o_t.T.astype(BF16)


def mla_attention(q_hm, k_hm, vt_hm, *, n_seq, seq_len, tq, tk, unroll):
    heads, n, _ = q_hm.shape
    nq = seq_len // tq
    return pl.pallas_call(
        functools.partial(_mla_attn_kernel, tk=tk, unroll=unroll),
        grid=(n_seq, heads, nq),
        in_specs=[
            pl.BlockSpec((1, tq, MLA_QK), lambda s, h, i: (h, s * nq + i, 0)),
            pl.BlockSpec((1, seq_len, MLA_QK), lambda s, h, i: (h, s, 0)),
            pl.BlockSpec((1, MLA_VT_ROWS, seq_len), lambda s, h, i: (h, 0, s)),
        ],
        out_specs=pl.BlockSpec((1, tq, MLA_V), lambda s, h, i: (h, s * nq + i, 0)),
        out_shape=jax.ShapeDtypeStruct((heads, n, MLA_V), BF16),
        scratch_shapes=[pltpu.VMEM((1, tq), F32), pltpu.VMEM((MLA_VT_ROWS, tq), F32),
                        pltpu.VMEM((tk, tq), F32), pltpu.VMEM((tk, tq), F32),
                        pltpu.VMEM((1, tq), F32), pltpu.VMEM((1, tq), F32)],
        compiler_params=_params("parallel", "parallel", "arbitrary"),
        name="mla_attention",
    )(q_hm, k_hm, vt_hm)


def rope_table(seq_len):
    inv_freq = ROPE_THETA ** (-jnp.arange(0, MLA_ROPE, 2, dtype=F32) / MLA_ROPE)
    ang = jnp.arange(seq_len, dtype=F32)[:, None] * inv_freq[None, :]
    cos, sin = jnp.cos(ang), jnp.sin(ang)
    return jnp.concatenate([cos, cos, -sin, sin], axis=-1)


def _swap_halves(w):
    half = w.shape[-1] // 2
    return jnp.concatenate([w[..., half:], w[..., :half]], axis=-1)


def trunk(x_parts, seq_len, p, tiles):
    d = x_parts[0].shape[1]
    n = sum(x.shape[0] for x in x_parts)
    n_seq = n // seq_len
    rows = seq_len // GRID_W
    row2 = lambda v: v.reshape(1, -1)

    w_qkv = p["na_w_qkv"][0].astype(BF16)
    na_dh = d // NA_HEADS
    col_scale = jnp.concatenate([jnp.full((d,), LOG2_E * na_dh ** -0.5, F32), jnp.ones((2 * d,), F32)])
    h = input_norm(x_parts, row2(p["g_mix"][0]), tm=tiles["tm"])
    qkv = qkv_proj(h, w_qkv, row2(p["na_b_qkv"][0]), row2(col_scale), tm=tiles["qkv_tm"], tn=tiles["qkv_tn"])
    bias = na_bias_table(p["na_rpb"][0], rows)
    o = na_attention(qkv, bias, n_seq=n_seq, seq_len=seq_len)
    x, h = out_proj(o, p["na_w_o"][0].astype(BF16), x_parts, row2(p["g_ffn"][0]), tm=tiles["tm"])
    a = _ffn_up(h, p, 0, tiles)
    x, h = ffn_down(a, p["ffn_w_down"][0].astype(BF16), x, row2(p["g_mix"][1]), tm=tiles["down_tm"])

    w_kr = p["mla_w_kr"][0]
    w_down = jnp.concatenate([p["mla_w_dq"][0], p["mla_w_dkv"][0], w_kr, _swap_halves(w_kr)], axis=1).astype(BF16)
    c2 = rope_table(seq_len)
    cq, ckv, kr = mla_down(h, w_down, row2(p["mla_g_q"][0]), row2(p["mla_g_kv"][0]), c2,
                           tm=tiles["tm"], seq_len=seq_len)
    q_rank = cq.shape[1]
    w_uq = p["mla_w_uq"][0].reshape(q_rank, MLA_HEADS, MLA_NOPE + MLA_ROPE)
    w_uq = jnp.concatenate([w_uq, _swap_halves(w_uq[..., MLA_NOPE:])], axis=-1)
    w_uq = w_uq.reshape(q_rank, MLA_HEADS * MLA_QK).astype(BF16)
    q_hm = mla_q_proj(cq, w_uq, c2, tm=tiles["tm"], heads_per_step=tiles["q_hps"], seq_len=seq_len,
                      scale=LOG2_E * (MLA_NOPE + MLA_ROPE) ** -0.5)
    kv_rank = ckv.shape[1]
    w_ukv = p["mla_w_ukv"][0].reshape(kv_rank, MLA_HEADS, MLA_NOPE + MLA_V)
    w_uk = w_ukv[..., :MLA_NOPE].reshape(kv_rank, MLA_HEADS * MLA_NOPE).astype(BF16)
    w_uvt = w_ukv[..., MLA_NOPE:].reshape(kv_rank, MLA_HEADS * MLA_V).T.astype(BF16)
    k_hm, vt_hm = mla_kv_proj(ckv, kr, w_uk, w_uvt, tm=tiles["tm"], heads_per_step=tiles["kv_hps"])
    o = mla_attention(q_hm, k_hm, vt_hm, n_seq=n_seq, seq_len=seq_len, tq=tiles["mla_tq"], tk=tiles["mla_tk"],
                      unroll=tiles["mla_unroll"])
    x, h = out_proj(o, p["mla_w_o"][0].astype(BF16), [x], row2(p["g_ffn"][1]), tm=tiles["tm"])
    a = _ffn_up(h, p, 1, tiles)
    w_down = p["ffn_w_down"][1].astype(BF16)
    outs, row_start = [], 0
    for part in x_parts:
        outs.append(ffn_down(a, w_down, x, row2(p["g_final"]), tm=tiles["down_tm"], final=True,
                             row_start=row_start, n_rows=part.shape[0]))
        row_start += part.shape[0]
    return outs


def _ffn_up(h, p, layer, tiles):
    return ffn_up(h, p["ffn_w_gate"][layer].astype(BF16), p["ffn_w_up"][layer].astype(BF16),
                  tm=tiles["ffn_tm"], tn=tiles["ffn_tn"])


TILES = dict(tm=512, qkv_tm=1024, qkv_tn=1024, ffn_tm=1024, ffn_tn=512, down_tm=256, q_hps=16, kv_hps=16,
             mla_tq=2048, mla_tk=1024, mla_unroll=6)


def kernel(x_prompt, x_sample, g_mix, g_ffn, g_final, na_w_qkv, na_b_qkv, na_rpb, na_w_o, mla_w_dq, mla_g_q,
           mla_w_uq, mla_w_dkv, mla_g_kv, mla_w_kr, mla_w_ukv, mla_w_o, ffn_w_gate, ffn_w_up, ffn_w_down):
    p = dict(g_mix=g_mix, g_ffn=g_ffn, g_final=g_final, na_w_qkv=na_w_qkv, na_b_qkv=na_b_qkv, na_rpb=na_rpb,
             na_w_o=na_w_o, mla_w_dq=mla_w_dq, mla_g_q=mla_g_q, mla_w_uq=mla_w_uq, mla_w_dkv=mla_w_dkv,
             mla_g_kv=mla_g_kv, mla_w_kr=mla_w_kr, mla_w_ukv=mla_w_ukv, mla_w_o=mla_w_o,
             ffn_w_gate=ffn_w_gate, ffn_w_up=ffn_w_up, ffn_w_down=ffn_w_down)
    bp, seq_len, d = x_prompt.shape
    bs = x_sample.shape[0]
    assert x_sample.shape[1:] == (seq_len, d)
    x_parts = [x_prompt.reshape(bp * seq_len, d), x_sample.reshape(bs * seq_len, d)]
    y_prompt, y_sample = trunk(x_parts, seq_len, p, TILES)
    return (y_prompt.reshape(bp, seq_len, d), y_sample.reshape(bs, seq_len, d))
```
